```python
import math
import jax, jax.numpy as jnp
from jax import lax
import numpy as np

D_MODEL = 4096
BATCH = 1
SEQ = 8192
DEPTH = 4

N_META = 16
BLOCK = 128
PAD_FRONT = BLOCK - N_META

SB_HEADS = 16
SB_HEAD_DIM = 128
SB_W = SB_HEADS * SB_HEAD_DIM

MLA_HEADS = 16
MLA_Q_RANK = 1024
MLA_KV_RANK = 512
MLA_NOPE = 128
MLA_ROPE = 64
MLA_V = 128
ROPE_THETA = 10000.0

SSM_HEADS = 32
SSM_HEAD_DIM = 64
SSM_D_INNER = SSM_HEADS * SSM_HEAD_DIM
SSM_GROUPS = 4
SSM_STATE = 128
SSM_CONV = 4
SSM_XBC = SSM_D_INNER + 2 * SSM_GROUPS * SSM_STATE
DT_MIN = 0.001
DT_MAX = 0.1

N_BRANCH = 3
BRANCH_W = 2048

D_FF = 6144
FFN_CONV = 3

LN_EPS = 1e-5
RMS_EPS = 1e-6
DEEPNORM_ALPHA = (2 * DEPTH) ** 0.25
DEEPNORM_BETA = (8 * DEPTH) ** -0.25
NEG_INF = -1e30

_SIZES = (SB_W, SB_W, SB_W,
          MLA_Q_RANK, MLA_KV_RANK, MLA_ROPE,
          SSM_D_INNER, SSM_XBC, SSM_HEADS,
          N_BRANCH * D_MODEL)
SPLIT_IDX = tuple(int(s) for s in np.cumsum(_SIZES)[:-1])
W_IN_COLS = int(sum(_SIZES))

kernel_name = "hybrid_sb_mla_ssd_deepnorm_block"


def layer_norm(x, g, b):
    xf = x.astype(jnp.float32)
    mu = jnp.mean(xf, axis=-1, keepdims=True)
    var = jnp.mean(jnp.square(xf - mu), axis=-1, keepdims=True)
    return ((xf - mu) * lax.rsqrt(var + LN_EPS) * g.astype(jnp.float32) + b.astype(jnp.float32)).astype(x.dtype)


def rms_norm(x, g):
    xf = x.astype(jnp.float32)
    ms = jnp.mean(jnp.square(xf), axis=-1, keepdims=True)
    return (xf * lax.rsqrt(ms + RMS_EPS) * g.astype(jnp.float32)).astype(x.dtype)


def causal_dwconv(x, w, b):
    k = w.shape[0]
    y = lax.conv_general_dilated(x, w[:, None, :].astype(x.dtype), window_strides=(1,),
                                 padding=[(k - 1, 0)],
                                 dimension_numbers=('NWC', 'WIO', 'NWC'),
                                 feature_group_count=x.shape[-1])
    return y + b.astype(x.dtype)


def apply_rope(x, cos, sin):
    xf = x.astype(jnp.float32)
    x1, x2 = jnp.split(xf, 2, axis=-1)
    c = cos[None, :, None, :]
    s = sin[None, :, None, :]
    return jnp.concatenate([x1 * c - x2 * s, x2 * c + x1 * s], axis=-1).astype(x.dtype)


def to_blocks(t):
    b, l, h, d = t.shape
    return t.reshape(b, l // BLOCK, BLOCK, h, d).transpose(1, 0, 3, 2, 4)


def from_blocks(o):
    nb, b, h, q, d = o.shape
    return o.transpose(1, 0, 3, 2, 4).reshape(b, nb * q, h * d)


def stick_breaking_attention(q, k, v, valid):
    l = q.shape[1]
    scale = SB_HEAD_DIM ** -0.5
    kh = k.transpose(0, 2, 1, 3)
    vh = v.transpose(0, 2, 1, 3)
    kpos = jnp.arange(l)
    qpos = kpos.reshape(-1, BLOCK)

    def one_block(args):
        qblk, qp = args
        z = jnp.einsum('bhqd,bhkd->bhqk', qblk, kh).astype(jnp.float32) * scale
        allowed = (kpos[None, :] < qp[:, None]) & valid[None, :]
        log_stay = jnp.where(allowed, jax.nn.log_sigmoid(-z), 0.0)
        later = lax.cumsum(log_stay, axis=3, reverse=True) - log_stay
        w = jnp.where(allowed, jnp.exp(jax.nn.log_sigmoid(z) + later), 0.0)
        return jnp.einsum('bhqk,bhkd->bhqd', w.astype(vh.dtype), vh)

    o = lax.map(one_block, (to_blocks(q), qpos))
    return from_blocks(o)


def mla_attention(q_nope, q_pe, k_nope, k_pe, v, valid):
    l = q_nope.shape[1]
    scale = (MLA_NOPE + MLA_ROPE) ** -0.5
    kn = k_nope.transpose(0, 2, 1, 3)
    vh = v.transpose(0, 2, 1, 3)
    kpos = jnp.arange(l)
    qpos = kpos.reshape(-1, BLOCK)

    def one_block(args):
        qn, qr, qp = args
        s = (jnp.einsum('bhqd,bhkd->bhqk', qn, kn)
             + jnp.einsum('bhqd,bkd->bhqk', qr, k_pe)).astype(jnp.float32) * scale
        allowed = (kpos[None, :] <= qp[:, None]) & valid[None, :]
        p = jax.nn.softmax(jnp.where(allowed, s, NEG_INF), axis=-1)
        return jnp.einsum('bhqk,bhkd->bhqd', p.astype(vh.dtype), vh)

    o = lax.map(one_block, (to_blocks(q_nope), to_blocks(q_pe), qpos))
    return from_blocks(o)


def ssd_chunked(xdt, adt, bm, cm):
    b, l, h, p = xdt.shape
    g, n = SSM_GROUPS, SSM_STATE
    r = h // g
    nc = l // BLOCK
    x = xdt.reshape(b, nc, BLOCK, g, r, p)
    a = adt.reshape(b, nc, BLOCK, g, r).transpose(0, 3, 4, 1, 2)
    bc = bm.reshape(b, nc, BLOCK, g, n)
    cc = cm.reshape(b, nc, BLOCK, g, n)
    a_cum = jnp.cumsum(a, axis=-1)
    causal = jnp.tril(jnp.ones((BLOCK, BLOCK), dtype=bool))
    seg = a_cum[..., :, None] - a_cum[..., None, :]
    lmat = jnp.exp(jnp.where(causal, seg, -jnp.inf))
    cb = jnp.einsum('bclgn,bcsgn->bcgls', cc, bc)
    y_diag = jnp.einsum('bcgls,bgrcls,bcsgrp->bclgrp', cb, lmat, x)
    decay_states = jnp.exp(a_cum[..., -1:] - a_cum)
    states = jnp.einsum('bclgn,bgrcl,bclgrp->bcgrpn', bc, decay_states, x).astype(jnp.float32)
    chunk_decay = jnp.exp(a_cum[..., -1])

    def step(hstate, inp):
        st, dec = inp
        return hstate * dec[..., None, None] + st, hstate

    init = jnp.zeros((b, g, r, p, n), jnp.float32)
    _, states_in = lax.scan(step, init, (jnp.moveaxis(states, 1, 0), jnp.moveaxis(chunk_decay, -1, 0)))
    states_in = jnp.moveaxis(states_in, 0, 1)
    y_off = jnp.einsum('bclgn,bcgrpn,bgrcl->bclgrp', cc, states_in, jnp.exp(a_cum))
    return (y_diag + y_off).reshape(b, l, h, p).astype(xdt.dtype)


def ssd_mixer(z, xbc, dt_raw, valid, conv_w, conv_b, dt_bias, a_log, d_skip, norm_g):
    b, l, _ = xbc.shape
    vmask = valid[None, :, None]
    xbc = jax.nn.silu(causal_dwconv(xbc * vmask.astype(xbc.dtype), conv_w, conv_b))
    xs, bm, cm = jnp.split(xbc, [SSM_D_INNER, SSM_D_INNER + SSM_GROUPS * SSM_STATE], axis=-1)
    dt = jax.nn.softplus(dt_raw.astype(jnp.float32) + dt_bias.astype(jnp.float32)) * vmask
    a = -jnp.exp(a_log.astype(jnp.float32))
    xh = xs.reshape(b, l, SSM_HEADS, SSM_HEAD_DIM)
    y = ssd_chunked(xh * dt[..., None].astype(xh.dtype), a * dt,
                    bm.reshape(b, l, SSM_GROUPS, SSM_STATE), cm.reshape(b, l, SSM_GROUPS, SSM_STATE))
    y = y + xh * d_skip[:, None].astype(xh.dtype)
    y = y.reshape(b, l, SSM_D_INNER) * jax.nn.silu(z)
    gs = SSM_D_INNER // SSM_GROUPS
    y = rms_norm(y.reshape(b, l, SSM_GROUPS, gs), norm_g.reshape(SSM_GROUPS, gs))
    return y.reshape(b, l, SSM_D_INNER)


def hybrid_mixer(h, valid, cos, sin, w_in, mla_q_norm, mla_kv_norm, w_uq, w_ukv,
                 ssm_conv_w, ssm_conv_b, ssm_dt_bias, ssm_a_log, ssm_d, ssm_norm, w_br, w_o):
    b, l, _ = h.shape
    proj = h @ w_in
    (q_sb, k_sb, v_sb, c_q, c_kv, k_pe, z, xbc, dt_raw, gates) = jnp.split(proj, SPLIT_IDX, axis=-1)
    sb_shape = (b, l, SB_HEADS, SB_HEAD_DIM)
    o_a = stick_breaking_attention(q_sb.reshape(sb_shape), k_sb.reshape(sb_shape), v_sb.reshape(sb_shape), valid)
    q = (rms_norm(c_q, mla_q_norm) @ w_uq).reshape(b, l, MLA_HEADS, MLA_NOPE + MLA_ROPE)
    q_nope, q_pe = jnp.split(q, [MLA_NOPE], axis=-1)
    q_pe = apply_rope(q_pe, cos, sin)
    kv = (rms_norm(c_kv, mla_kv_norm) @ w_ukv).reshape(b, l, MLA_HEADS, MLA_NOPE + MLA_V)
    k_nope, v_mla = jnp.split(kv, [MLA_NOPE], axis=-1)
    k_pe = apply_rope(k_pe[:, :, None, :], cos, sin)[:, :, 0, :]
    o_b = mla_attention(q_nope, q_pe, k_nope, k_pe, v_mla, valid)
    o_c = ssd_mixer(z, xbc, dt_raw, valid, ssm_conv_w, ssm_conv_b, ssm_dt_bias, ssm_a_log, ssm_d, ssm_norm)
    g = jax.nn.sigmoid(gates.astype(jnp.float32)).astype(h.dtype).reshape(b, l, N_BRANCH, D_MODEL)
    merged = (g[:, :, 0] * (o_a @ w_br[0])
              + g[:, :, 1] * (o_b @ w_br[1])
              + g[:, :, 2] * (o_c @ w_br[2]))
    return merged @ w_o


def conv_glu_ffn(h, valid, w_up, conv_w, conv_b, w_down):
    u = (h @ w_up) * valid[None, :, None].astype(h.dtype)
    u = causal_dwconv(u, conv_w, conv_b)
    gate, val = jnp.split(u, 2, axis=-1)
    return (jax.nn.silu(gate) * val) @ w_down


def setup_inputs(seed: int = 0) -> dict:
    key = jax.random.key(seed)
    ks = jax.random.split(key, 26)
    f32 = jnp.float32

    def nrm(k, shape, scale):
        return jax.random.normal(k, shape, f32) * scale

    dt0 = jnp.exp(jax.random.uniform(ks[11], (DEPTH, SSM_HEADS), f32)
                  * (math.log(DT_MAX) - math.log(DT_MIN)) + math.log(DT_MIN))
    return {
        "x": nrm(ks[0], (BATCH, SEQ, D_MODEL), 1.0),
        "meta_tokens": nrm(ks[1], (N_META, D_MODEL), 1.0),
        "ln_in_g": 1.0 + nrm(ks[2], (D_MODEL,), 0.02),
        "ln_in_b": nrm(ks[3], (D_MODEL,), 0.02),
        "w_in": nrm(ks[4], (DEPTH, D_MODEL, W_IN_COLS), D_MODEL ** -0.5),
        "mla_q_norm": 1.0 + nrm(ks[5], (DEPTH, MLA_Q_RANK), 0.02),
        "mla_kv_norm": 1.0 + nrm(ks[6], (DEPTH, MLA_KV_RANK), 0.02),
        "w_uq": nrm(ks[7], (DEPTH, MLA_Q_RANK, MLA_HEADS * (MLA_NOPE + MLA_ROPE)), MLA_Q_RANK ** -0.5),
        "w_ukv": nrm(ks[8], (DEPTH, MLA_KV_RANK, MLA_HEADS * (MLA_NOPE + MLA_V)), MLA_KV_RANK ** -0.5),
        "ssm_conv_w": nrm(ks[9], (DEPTH, SSM_CONV, SSM_XBC), SSM_CONV ** -0.5),
        "ssm_conv_b": nrm(ks[10], (DEPTH, SSM_XBC), 0.02),
        "ssm_dt_bias": dt0 + jnp.log(-jnp.expm1(-dt0)),
        "ssm_a_log": jnp.log(jax.random.uniform(ks[12], (DEPTH, SSM_HEADS), f32, 1.0, 16.0)),
        "ssm_d": 1.0 + nrm(ks[13], (DEPTH, SSM_HEADS), 0.02),
        "ssm_norm": 1.0 + nrm(ks[14], (DEPTH, SSM_D_INNER), 0.02),
        "w_br": nrm(ks[15], (DEPTH, N_BRANCH, BRANCH_W, D_MODEL), BRANCH_W ** -0.5 * DEEPNORM_BETA),
        "w_o": nrm(ks[16], (DEPTH, D_MODEL, D_MODEL), D_MODEL ** -0.5 * DEEPNORM_BETA),
        "ln1_g": 1.0 + nrm(ks[17], (DEPTH, D_MODEL), 0.02),
        "ln1_b": nrm(ks[18], (DEPTH, D_MODEL), 0.02),
        "w_up": nrm(ks[19], (DEPTH, D_MODEL, 2 * D_FF), D_MODEL ** -0.5),
        "ffn_conv_w": nrm(ks[20], (DEPTH, FFN_CONV, 2 * D_FF), FFN_CONV ** -0.5),
        "ffn_conv_b": nrm(ks[21], (DEPTH, 2 * D_FF), 0.02),
        "w_down": nrm(ks[22], (DEPTH, D_FF, D_MODEL), D_FF ** -0.5 * DEEPNORM_BETA),
        "ln2_g": 1.0 + nrm(ks[23], (DEPTH, D_MODEL), 0.02),
        "ln2_b": nrm(ks[24], (DEPTH, D_MODEL), 0.02),
    }


def reference(x, meta_tokens, ln_in_g, ln_in_b, w_in, mla_q_norm, mla_kv_norm, w_uq, w_ukv,
              ssm_conv_w, ssm_conv_b, ssm_dt_bias, ssm_a_log, ssm_d, ssm_norm, w_br, w_o,
              ln1_g, ln1_b, w_up, ffn_conv_w, ffn_conv_b, w_down, ln2_g, ln2_b):
    b = x.shape[0]
    pad = jnp.zeros((b, PAD_FRONT, D_MODEL), x.dtype)
    meta = jnp.broadcast_to(meta_tokens[None].astype(x.dtype), (b, N_META, D_MODEL))
    h = jnp.concatenate([pad, meta, x], axis=1)
    l = h.shape[1]
    idx = jnp.arange(l)
    valid = idx >= PAD_FRONT
    pos = jnp.maximum(idx - PAD_FRONT, 0).astype(jnp.float32)
    inv_freq = 1.0 / (ROPE_THETA ** (jnp.arange(0, MLA_ROPE, 2, dtype=jnp.float32) / MLA_ROPE))
    ang = pos[:, None] * inv_freq[None, :]
    cos, sin = jnp.cos(ang), jnp.sin(ang)
    h = layer_norm(h, ln_in_g, ln_in_b)
    for i in range(DEPTH):
        mix = hybrid_mixer(h, valid, cos, sin, w_in[i], mla_q_norm[i], mla_kv_norm[i], w_uq[i], w_ukv[i],
                           ssm_conv_w[i], ssm_conv_b[i], ssm_dt_bias[i], ssm_a_log[i], ssm_d[i], ssm_norm[i],
                           w_br[i], w_o[i])
        h = layer_norm(DEEPNORM_ALPHA * h + mix, ln1_g[i], ln1_b[i])
        f = conv_glu_ffn(h, valid, w_up[i], ffn_conv_w[i], ffn_conv_b[i], w_down[i])
        h = layer_norm(DEEPNORM_ALPHA * h + f, ln2_g[i], ln2_b[i])
    return h[:, PAD_FRONT + N_META:]
```

```python
import functools
import math

import jax
import jax.numpy as jnp
from jax import lax
from jax.experimental import pallas as pl
from jax.experimental.pallas import tpu as pltpu

f32 = jnp.float32
bf16 = jnp.bfloat16

BLOCK = 128
N_META = 16
PAD_FRONT = BLOCK - N_META
HEAD_DIM = 128
ROPE_DIM = 64
SSM_HEAD_DIM = 64
SSM_GROUPS = 4
SSM_STATE = 128
ROPE_THETA = 10000.0
LN_EPS = 1e-5
RMS_EPS = 1e-6
NEG_INF = -1e30
ATT_TILE = 256

V7X_VMEM_BYTES = 64 * 1024 * 1024
VMEM_INTERNAL_SCRATCH = 8 * 1024 * 1024


def _vmem_limit(block_bytes):
    return int(min(2 * block_bytes + VMEM_INTERNAL_SCRATCH, V7X_VMEM_BYTES - 4 * 1024 * 1024))


def _pick(n, cands):
    for c in cands:
        if n % c == 0:
            return c
    raise ValueError(f"no tile for {n}")


def _dot(a, b):
    return jnp.dot(a, b, preferred_element_type=f32)


def _dot_nt(a, b):
    return lax.dot_general(a, b, (((1,), (1,)), ((), ())), preferred_element_type=f32)


def _split3(x):
    hi = x.astype(bf16)
    r = x - hi.astype(f32)
    mid = r.astype(bf16)
    lo = (r - mid.astype(f32)).astype(bf16)
    return hi, mid, lo


def _dot3_lhs(x, w):
    hi, mid, lo = _split3(x)
    return _dot(hi, w) + _dot(mid, w) + _dot(lo, w)


def _dot3_rhs(w, x):
    hi, mid, lo = _split3(x)
    return _dot(w, hi) + _dot(w, mid) + _dot(w, lo)


def _sigmoid(x):
    return 1.0 / (1.0 + jnp.exp(-x))


def _softplus(x):
    return jnp.maximum(x, 0.0) + jnp.log(1.0 + jnp.exp(-jnp.abs(x)))


def _mm_kernel(a_ref, w_ref, o_ref):
    o_ref[...] = _dot(a_ref[...], w_ref[...]).astype(o_ref.dtype)


def _mm_res_kernel(a_ref, w_ref, r_ref, o_ref, *, alpha):
    o_ref[...] = (alpha * r_ref[...] + _dot(a_ref[...], w_ref[...])).astype(o_ref.dtype)


def _matmul(a, w, out_dtype, res=None, alpha=1.0, name="mm"):
    m, k = a.shape
    n = w.shape[1]
    tm = _pick(m, (640, 512, 384, 256, 128))
    tn = _pick(n, tuple(t for t in (1024, 512, 256, 128) if k * t * 2 <= 8 * 1024 * 1024))
    grid = (n // tn, m // tm)
    in_specs = [pl.BlockSpec((tm, k), lambda j, i: (i, 0)),
                pl.BlockSpec((k, tn), lambda j, i: (0, j))]
    args = [a, w]
    blk = tm * k * 2 + k * tn * 2 + tm * tn * jnp.dtype(out_dtype).itemsize
    if res is None:
        body = _mm_kernel
    else:
        body = functools.partial(_mm_res_kernel, alpha=alpha)
        in_specs.append(pl.BlockSpec((tm, tn), lambda j, i: (i, j)))
        args.append(res)
        blk += tm * tn * 4
    return pl.pallas_call(
        body, grid=grid, in_specs=in_specs,
        out_specs=pl.BlockSpec((tm, tn), lambda j, i: (i, j)),
        out_shape=jax.ShapeDtypeStruct((m, n), out_dtype),
        compiler_params=pltpu.CompilerParams(
            dimension_semantics=("parallel", "parallel"), vmem_limit_bytes=_vmem_limit(blk)),
        name=name,
    )(*args)


def _ln_kernel(x_ref, g_ref, b_ref, o_ref, ob_ref):
    x = x_ref[...]
    mu = jnp.mean(x, axis=-1, keepdims=True)
    xc = x - mu
    var = jnp.mean(xc * xc, axis=-1, keepdims=True)
    y = xc * lax.rsqrt(var + LN_EPS) * g_ref[...] + b_ref[...]
    o_ref[...] = y
    ob_ref[...] = y.astype(bf16)


def _layer_norm(x, g, b):
    m, d = x.shape
    tr = _pick(m, (320, 256, 128))
    blk = tr * d * (4 + 4 + 2)
    return pl.pallas_call(
        _ln_kernel, grid=(m // tr,),
        in_specs=[pl.BlockSpec((tr, d), lambda i: (i, 0)),
                  pl.BlockSpec((1, d), lambda i: (0, 0)),
                  pl.BlockSpec((1, d), lambda i: (0, 0))],
        out_specs=[pl.BlockSpec((tr, d), lambda i: (i, 0)),
                   pl.BlockSpec((tr, d), lambda i: (i, 0))],
        out_shape=[jax.ShapeDtypeStruct((m, d), f32), jax.ShapeDtypeStruct((m, d), bf16)],
        compiler_params=pltpu.CompilerParams(
            dimension_semantics=("parallel",), vmem_limit_bytes=_vmem_limit(blk)),
        name="layer_norm",
    )(x, g.reshape(1, d), b.reshape(1, d))


def _sb_block(q, k, v, us, carry, mode, scale):
    z = _dot_nt(q, k) * scale
    sp = _softplus(z)
    ls = -sp
    allowed = None
    if mode != "full":
        row = lax.broadcasted_iota(jnp.int32, z.shape, 0)
        col = lax.broadcasted_iota(jnp.int32, z.shape, 1)
        if mode == "diag":
            allowed = col < row
        elif mode == "meta":
            allowed = col >= PAD_FRONT
        else:
            allowed = (col < row) & (col >= PAD_FRONT)
        ls = jnp.where(allowed, ls, 0.0)
    hi = ls.astype(bf16)
    lo = (ls - hi.astype(f32)).astype(bf16)
    later = _dot(hi, us) + _dot(lo, us)
    w = jnp.exp((z - sp) + later + carry)
    if allowed is not None:
        w = jnp.where(allowed, w, 0.0)
    pv = _dot(w.astype(bf16), v)
    rs = jnp.sum(ls, axis=1, keepdims=True)
    return pv, rs


def _sb_kernel(q_ref, k_ref, v_ref, us_ref, o_ref, acc_ref, carry_ref, *, scale, n_tiles):
    t = ATT_TILE
    us = us_ref[...]
    us_meta = us_ref[0:BLOCK, 0:BLOCK]

    pv, _ = _sb_block(q_ref[0:BLOCK, :], k_ref[0:BLOCK, :], v_ref[0:BLOCK, :], us_meta,
                      jnp.zeros((BLOCK, BLOCK), f32), "metadiag", scale)
    o_ref[0:BLOCK, :] = pv.astype(o_ref.dtype)

    def q_body(m, c):
        q0 = pl.multiple_of(BLOCK + m * t, BLOCK)
        q = q_ref[pl.ds(q0, t), :]
        pv, rs = _sb_block(q, k_ref[pl.ds(q0, t), :], v_ref[pl.ds(q0, t), :], us,
                           jnp.zeros((t, t), f32), "diag", scale)
        acc_ref[...] = pv
        carry_ref[...] = jnp.broadcast_to(rs, (t, BLOCK))

        def k_body(jj, c2):
            k0 = pl.multiple_of(BLOCK + (m - 1 - jj) * t, BLOCK)
            cb = carry_ref[...]
            pv, rs = _sb_block(q, k_ref[pl.ds(k0, t), :], v_ref[pl.ds(k0, t), :], us,
                               jnp.concatenate([cb, cb], axis=1), "full", scale)
            acc_ref[...] += pv
            carry_ref[...] = cb + rs
            return c2

        lax.fori_loop(0, m, k_body, 0)
        pv, _ = _sb_block(q, k_ref[0:BLOCK, :], v_ref[0:BLOCK, :], us_meta,
                          carry_ref[...], "meta", scale)
        o_ref[pl.ds(q0, t), :] = (acc_ref[...] + pv).astype(o_ref.dtype)
        return c

    lax.fori_loop(0, n_tiles, q_body, 0)


def _sb_attention(qkv, heads):
    l = qkv.shape[0]
    n_tiles = (l - BLOCK) // ATT_TILE
    assert BLOCK + n_tiles * ATT_TILE == l
    idx = jnp.arange(ATT_TILE)
    us = (idx[:, None] > idx[None, :]).astype(bf16)
    blk = 4 * l * HEAD_DIM * 2 + ATT_TILE * ATT_TILE * 2
    return pl.pallas_call(
        functools.partial(_sb_kernel, scale=HEAD_DIM ** -0.5, n_tiles=n_tiles),
        grid=(heads,),
        in_specs=[pl.BlockSpec((l, HEAD_DIM), lambda h: (0, h)),
                  pl.BlockSpec((l, HEAD_DIM), lambda h: (0, heads + h)),
                  pl.BlockSpec((l, HEAD_DIM), lambda h: (0, 2 * heads + h)),
                  pl.BlockSpec((ATT_TILE, ATT_TILE), lambda h: (0, 0))],
        out_specs=pl.BlockSpec((l, HEAD_DIM), lambda h: (0, h)),
        out_shape=jax.ShapeDtypeStruct((l, heads * HEAD_DIM), bf16),
        scratch_shapes=[pltpu.VMEM((ATT_TILE, HEAD_DIM), f32), pltpu.VMEM((ATT_TILE, BLOCK), f32)],
        compiler_params=pltpu.CompilerParams(
            dimension_semantics=("parallel",), vmem_limit_bytes=_vmem_limit(blk)),
        name="sb_attention",
    )(qkv, qkv, qkv, us)


def _rope_mix(y2, cs):
    t = y2 * cs
    pe = t + pltpu.roll(t, ROPE_DIM, axis=1)
    lane = lax.broadcasted_iota(jnp.int32, pe.shape, 1)
    return jnp.where(lane < ROPE_DIM, pe, 0.0)


def _rms(x, g):
    ms = jnp.mean(x * x, axis=-1, keepdims=True)
    return x * lax.rsqrt(ms + RMS_EPS) * g


def _q_up_kernel(cq_ref, g_ref, w_ref, cs_ref, o_ref, xn_ref):
    @pl.when(pl.program_id(1) == 0)
    def _():
        xn_ref[...] = _rms(cq_ref[...], g_ref[...]).astype(bf16)

    y = _dot(xn_ref[...], w_ref[...])
    o_ref[:, 0:HEAD_DIM] = y[:, 0:HEAD_DIM].astype(bf16)
    o_ref[:, HEAD_DIM:] = _rope_mix(y[:, HEAD_DIM:], cs_ref[...]).astype(bf16)


def _kv_up_kernel(ckv_ref, g_ref, wk_ref, wv_ref, kpe_ref, cs_ref, k_ref, v_ref, xn_ref):
    @pl.when(pl.program_id(1) == 0)
    def _():
        xn_ref[...] = _rms(ckv_ref[...], g_ref[...]).astype(bf16)

    xn = xn_ref[...]
    k_ref[:, 0:HEAD_DIM] = _dot(xn, wk_ref[...]).astype(bf16)
    k_ref[:, HEAD_DIM:] = _rope_mix(kpe_ref[...], cs_ref[...]).astype(bf16)
    v_ref[...] = _dot(xn, wv_ref[...]).astype(bf16)


def _mla_step(q, k, v, m_ref, l_ref, acc_ref, mode, scale):
    s = _dot_nt(q, k) * scale
    if mode != "full":
        row = lax.broadcasted_iota(jnp.int32, s.shape, 0)
        col = lax.broadcasted_iota(jnp.int32, s.shape, 1)
        if mode == "diag":
            allowed = col <= row
        elif mode == "meta":
            allowed = col >= PAD_FRONT
        else:
            allowed = (col <= row) & (col >= PAD_FRONT)
        s = jnp.where(allowed, s, NEG_INF)
    m_old = m_ref[...]
    m_new = jnp.maximum(m_old, jnp.max(s, axis=1, keepdims=True))
    p = jnp.exp(s - m_new)
    corr = jnp.exp(m_old - m_new)
    l_ref[...] = corr * l_ref[...] + jnp.sum(p, axis=1, keepdims=True)
    acc_ref[...] = corr * acc_ref[...] + _dot(p.astype(bf16), v)
    m_ref[...] = m_new


def _mla_kernel(q_ref, k_ref, v_ref, o_ref, m_ref, l_ref, acc_ref, m0_ref, l0_ref, acc0_ref, *, scale, n_tiles):
    t = ATT_TILE

    m0_ref[...] = jnp.full(m0_ref.shape, NEG_INF, f32)
    l0_ref[...] = jnp.zeros(l0_ref.shape, f32)
    acc0_ref[...] = jnp.zeros(acc0_ref.shape, f32)
    _mla_step(q_ref[0:BLOCK, :], k_ref[0:BLOCK, :], v_ref[0:BLOCK, :], m0_ref, l0_ref, acc0_ref,
              "metadiag", scale)
    o_ref[0:BLOCK, :] = (acc0_ref[...] / l0_ref[...]).astype(o_ref.dtype)

    def q_body(mi, c):
        q0 = pl.multiple_of(BLOCK + mi * t, BLOCK)
        q = q_ref[pl.ds(q0, t), :]
        m_ref[...] = jnp.full(m_ref.shape, NEG_INF, f32)
        l_ref[...] = jnp.zeros(l_ref.shape, f32)
        acc_ref[...] = jnp.zeros(acc_ref.shape, f32)
        _mla_step(q, k_ref[0:BLOCK, :], v_ref[0:BLOCK, :], m_ref, l_ref, acc_ref, "meta", scale)

        def k_body(j, c2):
            k0 = pl.multiple_of(BLOCK + j * t, BLOCK)
            _mla_step(q, k_ref[pl.ds(k0, t), :], v_ref[pl.ds(k0, t), :], m_ref, l_ref, acc_ref,
                      "full", scale)
            return c2

        lax.fori_loop(0, mi, k_body, 0)
        _mla_step(q, k_ref[pl.ds(q0, t), :], v_ref[pl.ds(q0, t), :], m_ref, l_ref, acc_ref,
                  "diag", scale)
        o_ref[pl.ds(q0, t), :] = (acc_ref[...] / l_ref[...]).astype(o_ref.dtype)
        return c

    lax.fori_loop(0, n_tiles, q_body, 0)


def _mla_branch(cqkv, kpe, cs, q_norm, kv_norm, w_q, w_k, w_v):
    l = cqkv.shape[0]
    heads, q_rank, _ = w_q.shape
    kv_rank = w_k.shape[1]
    assert q_rank % kv_rank == 0
    qk_dim = 2 * HEAD_DIM
    tm = _pick(l, (640, 512, 384, 256, 128))
    grid = (l // tm, heads)
    q_cat = pl.pallas_call(
        _q_up_kernel, grid=grid,
        in_specs=[pl.BlockSpec((tm, q_rank), lambda i, h: (i, 0)),
                  pl.BlockSpec((1, q_rank), lambda i, h: (0, 0)),
                  pl.BlockSpec((None, q_rank, qk_dim), lambda i, h: (h, 0, 0)),
                  pl.BlockSpec((tm, BLOCK), lambda i, h: (i, 0))],
        out_specs=pl.BlockSpec((None, tm, qk_dim), lambda i, h: (h, i, 0)),
        out_shape=jax.ShapeDtypeStruct((heads, l, qk_dim), bf16),
        scratch_shapes=[pltpu.VMEM((tm, q_rank), bf16)],
        compiler_params=pltpu.CompilerParams(
            dimension_semantics=("parallel", "arbitrary"),
            vmem_limit_bytes=_vmem_limit(tm * q_rank * 6 + q_rank * qk_dim * 2 + tm * qk_dim * 2 + tm * BLOCK * 4)),
        name="mla_q_up",
    )(cqkv, q_norm.reshape(1, q_rank), w_q, cs)
    k_cat, v = pl.pallas_call(
        _kv_up_kernel, grid=grid,
        in_specs=[pl.BlockSpec((tm, kv_rank), lambda i, h: (i, q_rank // kv_rank)),
                  pl.BlockSpec((1, kv_rank), lambda i, h: (0, 0)),
                  pl.BlockSpec((None, kv_rank, HEAD_DIM), lambda i, h: (h, 0, 0)),
                  pl.BlockSpec((None, kv_rank, HEAD_DIM), lambda i, h: (h, 0, 0)),
                  pl.BlockSpec((tm, BLOCK), lambda i, h: (i, 0)),
                  pl.BlockSpec((tm, BLOCK), lambda i, h: (i, 0))],
        out_specs=[pl.BlockSpec((None, tm, qk_dim), lambda i, h: (h, i, 0)),
                   pl.BlockSpec((tm, HEAD_DIM), lambda i, h: (i, h))],
        out_shape=[jax.ShapeDtypeStruct((heads, l, qk_dim), bf16),
                   jax.ShapeDtypeStruct((l, heads * HEAD_DIM), bf16)],
        scratch_shapes=[pltpu.VMEM((tm, kv_rank), bf16)],
        compiler_params=pltpu.CompilerParams(
            dimension_semantics=("parallel", "arbitrary"),
            vmem_limit_bytes=_vmem_limit(tm * kv_rank * 6 + 2 * kv_rank * HEAD_DIM * 2 + tm * BLOCK * 8
                                         + tm * (qk_dim + HEAD_DIM) * 2)),
        name="mla_kv_up",
    )(cqkv, kv_norm.reshape(1, kv_rank), w_k, w_v, kpe, cs)

    n_tiles = (l - BLOCK) // ATT_TILE
    blk = 2 * l * qk_dim * 2 + 2 * l * HEAD_DIM * 2
    return pl.pallas_call(
        functools.partial(_mla_kernel, scale=(HEAD_DIM + ROPE_DIM) ** -0.5, n_tiles=n_tiles),
        grid=(heads,),
        in_specs=[pl.BlockSpec((None, l, qk_dim), lambda h: (h, 0, 0)),
                  pl.BlockSpec((None, l, qk_dim), lambda h: (h, 0, 0)),
                  pl.BlockSpec((l, HEAD_DIM), lambda h: (0, h))],
        out_specs=pl.BlockSpec((l, HEAD_DIM), lambda h: (0, h)),
        out_shape=jax.ShapeDtypeStruct((l, heads * HEAD_DIM), bf16),
        scratch_shapes=[pltpu.VMEM((ATT_TILE, 1), f32), pltpu.VMEM((ATT_TILE, 1), f32),
                        pltpu.VMEM((ATT_TILE, HEAD_DIM), f32),
                        pltpu.VMEM((BLOCK, 1), f32), pltpu.VMEM((BLOCK, 1), f32),
                        pltpu.VMEM((BLOCK, HEAD_DIM), f32)],
        compiler_params=pltpu.CompilerParams(
            dimension_semantics=("parallel",), vmem_limit_bytes=_vmem_limit(blk)),
        name="mla_attention",
    )(q_cat, k_cat, v)


def _ssd_kernel(xbc_ref, xprev_ref, z_ref, dtr_ref, dtrt_ref, cw_ref, cb_ref, dtb_ref, dtbt_ref,
                alog_ref, alogt_ref, dexp_ref, ng_ref, e64_ref, e128_ref, tril_ref, triu_ref,
                o_ref, ht_ref, xs_ref, bc_ref, xdt_ref, xdec_ref, y_ref, *, d_inner, heads, conv_k):
    c = pl.program_id(0)
    n = SSM_STATE
    pairs = heads // 2
    pairs_per_group = pairs // SSM_GROUPS
    gn = SSM_GROUPS * n

    @pl.when(c == 0)
    def _():
        ht_ref[...] = jnp.zeros(ht_ref.shape, f32)

    row = lax.broadcasted_iota(jnp.int32, (BLOCK, 1), 0)
    grow = c * BLOCK + row
    valid_r = grow >= PAD_FRONT

    x = jnp.where(valid_r, xbc_ref[...], 0.0)
    xp = jnp.where(grow - BLOCK >= PAD_FRONT, xprev_ref[...], 0.0)
    acc = x * cw_ref[conv_k - 1:conv_k, :] + cb_ref[...]
    for s in range(1, conv_k):
        shifted = jnp.where(row >= s, pltpu.roll(x, s, axis=0), pltpu.roll(xp, s, axis=0))
        acc = acc + shifted * cw_ref[conv_k - 1 - s:conv_k - s, :]
    xc = acc * _sigmoid(acc)
    xs_ref[...] = xc[:, 0:d_inner]
    bc_ref[...] = xc[:, d_inner:]

    dt = jnp.where(valid_r, _softplus(dtr_ref[...] + dtb_ref[...]), 0.0)
    adt = -jnp.exp(alog_ref[...]) * dt
    a_cum = _dot3_rhs(tril_ref[...], adt)
    col = lax.broadcasted_iota(jnp.int32, (1, BLOCK), 1)
    dtt = jnp.where(c * BLOCK + col >= PAD_FRONT, _softplus(dtrt_ref[...] + dtbt_ref[...]), 0.0)
    a_cum_t = _dot3_lhs(-jnp.exp(alogt_ref[...]) * dtt, triu_ref[...])

    e64 = e64_ref[...]
    dt64 = _dot3_lhs(dt, e64)
    ac64 = _dot3_lhs(a_cum, e64)
    ac128 = _dot3_lhs(a_cum, e128_ref[...])
    a_last = ac64[BLOCK - 1:BLOCK, :]
    xdt = xs_ref[...] * dt64
    xdt_ref[...] = xdt.astype(bf16)
    xdec_ref[...] = (xdt * jnp.exp(a_last - ac64)).astype(bf16)
    eac = jnp.exp(ac64)
    chunk_decay = jnp.exp(a_last)

    li = lax.broadcasted_iota(jnp.int32, (BLOCK, BLOCK), 0)
    si = lax.broadcasted_iota(jnp.int32, (BLOCK, BLOCK), 1)
    causal = li >= si
    low_half = si < SSM_HEAD_DIM
    for g in range(SSM_GROUPS):
        bg = bc_ref[:, g * n:(g + 1) * n]
        cg = bc_ref[:, gn + g * n:gn + (g + 1) * n].astype(bf16)
        bgt = bg.T.astype(bf16)
        cb = _dot(cg, bgt)
        for jp in range(pairs_per_group):
            j = g * pairs_per_group + jp
            lanes = slice(j * BLOCK, (j + 1) * BLOCK)
            xpair = xdt_ref[:, lanes]
            ys = []
            for hh in (2 * j, 2 * j + 1):
                seg = ac128[:, hh * BLOCK:(hh + 1) * BLOCK] - a_cum_t[hh:hh + 1, :]
                lm = jnp.exp(jnp.where(causal, seg, NEG_INF))
                ys.append(_dot((cb * lm).astype(bf16), xpair))
            y_diag = jnp.where(low_half, ys[0], ys[1])
            ht = ht_ref[j]
            y_off = _dot(cg, ht.astype(bf16)) * eac[:, lanes]
            ht_ref[j] = ht * chunk_decay[:, lanes] + _dot(bgt, xdec_ref[:, lanes])
            y_ref[:, lanes] = y_diag + y_off + xs_ref[:, lanes] * dexp_ref[:, lanes]

    zz = z_ref[...]
    y = y_ref[...] * (zz * _sigmoid(zz))
    gs = d_inner // SSM_GROUPS
    for g in range(SSM_GROUPS):
        yg = y[:, g * gs:(g + 1) * gs]
        o_ref[:, g * gs:(g + 1) * gs] = _rms(yg, ng_ref[:, g * gs:(g + 1) * gs]).astype(o_ref.dtype)


def _ssd_branch(z, xbc, dtr, dtr_t, conv_w, conv_b, dt_bias, a_log, d_skip, norm_g):
    l, d_inner = z.shape
    d_xbc = xbc.shape[1]
    heads = dt_bias.shape[0]
    conv_k = conv_w.shape[0]
    assert heads <= BLOCK and heads % (2 * SSM_GROUPS) == 0 and d_inner == heads * SSM_HEAD_DIM
    assert d_xbc == d_inner + 2 * SSM_GROUPS * SSM_STATE
    nc = l // BLOCK
    pad = BLOCK - heads
    dtb = jnp.pad(dt_bias, (0, pad))
    alog = jnp.pad(a_log, (0, pad))
    hid = jnp.arange(BLOCK)
    e64 = ((jnp.arange(d_inner)[None, :] // SSM_HEAD_DIM) == hid[:, None]).astype(bf16)
    e128 = ((jnp.arange(heads * BLOCK)[None, :] // BLOCK) == hid[:, None]).astype(bf16)
    tril = (hid[:, None] >= hid[None, :]).astype(bf16)
    triu = (hid[:, None] <= hid[None, :]).astype(bf16)
    full = lambda shape: pl.BlockSpec(shape, lambda c: (0,) * len(shape))
    blk = (2 * BLOCK * d_xbc * 4 + BLOCK * d_inner * 4 + BLOCK * d_inner * 2 + 4 * BLOCK * BLOCK * 4
           + conv_k * d_xbc * 4 + BLOCK * (d_inner + heads * BLOCK) * 2)
    scratch = [pltpu.VMEM((heads // 2, SSM_STATE, BLOCK), f32),
               pltpu.VMEM((BLOCK, d_inner), f32),
               pltpu.VMEM((BLOCK, d_xbc - d_inner), f32),
               pltpu.VMEM((BLOCK, d_inner), bf16),
               pltpu.VMEM((BLOCK, d_inner), bf16),
               pltpu.VMEM((BLOCK, d_inner), f32)]
    scratch_bytes = (heads // 2) * SSM_STATE * BLOCK * 4 + BLOCK * (d_inner * 12 + (d_xbc - d_inner) * 4)
    return pl.pallas_call(
        functools.partial(_ssd_kernel, d_inner=d_inner, heads=heads, conv_k=conv_k),
        grid=(nc,),
        in_specs=[pl.BlockSpec((BLOCK, d_xbc), lambda c: (c, 0)),
                  pl.BlockSpec((BLOCK, d_xbc), lambda c: (jnp.maximum(c - 1, 0), 0)),
                  pl.BlockSpec((BLOCK, d_inner), lambda c: (c, 0)),
                  pl.BlockSpec((BLOCK, BLOCK), lambda c: (c, 0)),
                  pl.BlockSpec((BLOCK, BLOCK), lambda c: (0, c)),
                  full((conv_k, d_xbc)), full((1, d_xbc)),
                  full((1, BLOCK)), full((BLOCK, 1)), full((1, BLOCK)), full((BLOCK, 1)),
                  full((1, d_inner)), full((1, d_inner)),
                  full((BLOCK, d_inner)), full((BLOCK, heads * BLOCK)),
                  full((BLOCK, BLOCK)), full((BLOCK, BLOCK))],
        out_specs=pl.BlockSpec((BLOCK, d_inner), lambda c: (c, 0)),
        out_shape=jax.ShapeDtypeStruct((l, d_inner), bf16),
        scratch_shapes=scratch,
        compiler_params=pltpu.CompilerParams(
            dimension_semantics=("arbitrary",), vmem_limit_bytes=_vmem_limit(blk + scratch_bytes)),
        name="ssd",
    )(xbc, xbc, z, dtr, dtr_t, conv_w, conv_b.reshape(1, d_xbc),
      dtb.reshape(1, BLOCK), dtb.reshape(BLOCK, 1), alog.reshape(1, BLOCK), alog.reshape(BLOCK, 1),
      jnp.repeat(d_skip, SSM_HEAD_DIM).reshape(1, d_inner), norm_g.reshape(1, d_inner),
      e64, e128, tril, triu)


def _merge_kernel(oa_ref, ob_ref, oc_ref, w_ref, ga_ref, gb_ref, gc_ref, o_ref):
    acc = _sigmoid(ga_ref[...]) * _dot(oa_ref[...], w_ref[0])
    acc = acc + _sigmoid(gb_ref[...]) * _dot(ob_ref[...], w_ref[1])
    acc = acc + _sigmoid(gc_ref[...]) * _dot(oc_ref[...], w_ref[2])
    o_ref[...] = acc.astype(o_ref.dtype)


def _merge(o_a, o_b, o_c, w_br, gates):
    l, bw = o_a.shape
    d = w_br.shape[2]
    tm = _pick(l, (640, 512, 384, 256, 128))
    tn = _pick(d, (512, 256, 128))
    nb = d // tn
    o_spec = pl.BlockSpec((tm, bw), lambda i, j: (i, 0))
    g_spec = lambda b: pl.BlockSpec((tm, tn), lambda i, j: (i, b * nb + j))
    blk = 3 * tm * bw * 2 + 3 * bw * tn * 2 + 3 * tm * tn * 4 + tm * tn * 2
    return pl.pallas_call(
        _merge_kernel, grid=(l // tm, nb),
        in_specs=[o_spec, o_spec, o_spec,
                  pl.BlockSpec((3, bw, tn), lambda i, j: (0, 0, j)),
                  g_spec(0), g_spec(1), g_spec(2)],
        out_specs=pl.BlockSpec((tm, tn), lambda i, j: (i, j)),
        out_shape=jax.ShapeDtypeStruct((l, d), bf16),
        compiler_params=pltpu.CompilerParams(
            dimension_semantics=("parallel", "parallel"), vmem_limit_bytes=_vmem_limit(blk)),
        name="merge",
    )(o_a, o_b, o_c, w_br, gates, gates, gates)


def _shift_rows(x, halo, s):
    r = pltpu.roll(x, s, axis=0)
    hr = pltpu.roll(halo, s, axis=0)
    row8 = lax.broadcasted_iota(jnp.int32, halo.shape, 0)
    top = jnp.where(row8 < s, hr, r[0:8, :])
    return jnp.concatenate([top, r[8:, :]], axis=0)


def _glu_kernel(ug_ref, uv_ref, hg_ref, hv_ref, wg_ref, wv_ref, bg_ref, bv_ref, o_ref, *, conv_k, tm):
    i = pl.program_id(0)
    row = lax.broadcasted_iota(jnp.int32, (tm, 1), 0) + i * tm
    row8 = lax.broadcasted_iota(jnp.int32, (8, 1), 0) + i * tm - 8

    def conv(u_ref, h_ref, w_ref, b_ref):
        u = jnp.where(row >= PAD_FRONT, u_ref[...], 0.0)
        halo = jnp.where(row8 >= PAD_FRONT, h_ref[...], 0.0)
        acc = u * w_ref[conv_k - 1:conv_k, :] + b_ref[...]
        for s in range(1, conv_k):
            acc = acc + _shift_rows(u, halo, s) * w_ref[conv_k - 1 - s:conv_k - s, :]
        return acc

    gate = conv(ug_ref, hg_ref, wg_ref, bg_ref)
    val = conv(uv_ref, hv_ref, wv_ref, bv_ref)
    o_ref[...] = (gate * _sigmoid(gate) * val).astype(o_ref.dtype)


def _conv_glu(u, conv_w, conv_b):
    l, two_ff = u.shape
    d_ff = two_ff // 2
    conv_k = conv_w.shape[0]
    assert conv_k <= 8
    tm = _pick(l, (640, 512, 384, 256, 128))
    tc = _pick(d_ff, (512, 256, 128))
    nb = d_ff // tc
    hb = tm // 8
    cb = conv_b.reshape(1, two_ff)
    halo = lambda off: pl.BlockSpec((8, tc), lambda i, j: (jnp.maximum(i * hb - 1, 0), off + j))
    blk = 2 * tm * tc * 4 + 2 * 8 * tc * 4 + tm * tc * 2 + 2 * (conv_k + 1) * tc * 4
    return pl.pallas_call(
        functools.partial(_glu_kernel, conv_k=conv_k, tm=tm),
        grid=(l // tm, nb),
        in_specs=[pl.BlockSpec((tm, tc), lambda i, j: (i, j)),
                  pl.BlockSpec((tm, tc), lambda i, j: (i, nb + j)),
                  halo(0), halo(nb),
                  pl.BlockSpec((conv_k, tc), lambda i, j: (0, j)),
                  pl.BlockSpec((conv_k, tc), lambda i, j: (0, nb + j)),
                  pl.BlockSpec((1, tc), lambda i, j: (0, j)),
                  pl.BlockSpec((1, tc), lambda i, j: (0, nb + j))],
        out_specs=pl.BlockSpec((tm, tc), lambda i, j: (i, j)),
        out_shape=jax.ShapeDtypeStruct((l, d_ff), bf16),
        compiler_params=pltpu.CompilerParams(
            dimension_semantics=("parallel", "parallel"), vmem_limit_bytes=_vmem_limit(blk)),
        name="conv_glu",
    )(u, u, u, u, conv_w, conv_w, cb, cb)


def _rotate_half_cols(w):
    half = w.shape[-1] // 2
    return jnp.concatenate([-w[..., half:], w[..., :half]], axis=-1)


def _layer(h, hb, cs, p, alpha):
    (w_in, q_norm, kv_norm, w_uq, w_ukv, conv_w, conv_b, dt_bias, a_log, d_skip, ssm_norm, w_br, w_o,
     ln1_g, ln1_b, w_up, f_conv_w, f_conv_b, w_down, ln2_g, ln2_b) = p
    d = h.shape[1]
    bw = w_br.shape[1]
    sb_heads = bw // HEAD_DIM
    q_rank, kv_rank = q_norm.shape[0], kv_norm.shape[0]
    ssm_heads = dt_bias.shape[0]
    d_inner = ssm_norm.shape[0]
    d_xbc = conv_w.shape[1]
    mla_heads = w_uq.shape[1] // (HEAD_DIM + ROPE_DIM)

    sizes = (bw, bw, bw, q_rank, kv_rank, ROPE_DIM, d_inner, d_xbc, ssm_heads, 3 * d)
    offs = [0]
    for s in sizes:
        offs.append(offs[-1] + s)
    assert offs[-1] == w_in.shape[1]
    seg = lambda i: w_in[:, offs[i]:offs[i + 1]]
    w_kpe = seg(5)
    w_misc = jnp.concatenate([w_kpe, _rotate_half_cols(w_kpe), seg(8),
                              jnp.zeros((d, BLOCK - ssm_heads), f32)], axis=1)
    qkv = _matmul(hb, w_in[:, :offs[3]].astype(bf16), bf16, name="in_qkv")
    cqkv = _matmul(hb, w_in[:, offs[3]:offs[5]].astype(bf16), f32, name="in_cqkv")
    misc = _matmul(hb, w_misc.astype(bf16), f32, name="in_misc")
    z = _matmul(hb, seg(6).astype(bf16), f32, name="in_z")
    xbc = _matmul(hb, seg(7).astype(bf16), f32, name="in_xbc")
    gates = _matmul(hb, seg(9).astype(bf16), f32, name="in_gates")

    o_a = _sb_attention(qkv, sb_heads)

    wq = w_uq.reshape(q_rank, mla_heads, HEAD_DIM + ROPE_DIM)
    wq_pe = wq[..., HEAD_DIM:]
    wq = jnp.concatenate([wq[..., :HEAD_DIM], wq_pe, _rotate_half_cols(wq_pe)], axis=-1)
    wq = wq.transpose(1, 0, 2).astype(bf16)
    wkv = w_ukv.reshape(kv_rank, mla_heads, 2 * HEAD_DIM).transpose(1, 0, 2).astype(bf16)
    o_b = _mla_branch(cqkv, misc[:, :BLOCK], cs, q_norm, kv_norm, wq, wkv[..., :HEAD_DIM], wkv[..., HEAD_DIM:])

    dtr = misc[:, BLOCK:]
    o_c = _ssd_branch(z, xbc, dtr, dtr.T, conv_w, conv_b, dt_bias, a_log, d_skip, ssm_norm)

    merged = _merge(o_a, o_b, o_c, w_br.astype(bf16), gates)
    s1 = _matmul(merged, w_o.astype(bf16), f32, res=h, alpha=alpha, name="out_proj")
    h, hb = _layer_norm(s1, ln1_g, ln1_b)

    u = _matmul(hb, w_up.astype(bf16), f32, name="ffn_up")
    act = _conv_glu(u, f_conv_w, f_conv_b)
    s2 = _matmul(act, w_down.astype(bf16), f32, res=h, alpha=alpha, name="ffn_down")
    return _layer_norm(s2, ln2_g, ln2_b)


def kernel(x, meta_tokens, ln_in_g, ln_in_b, w_in, mla_q_norm, mla_kv_norm, w_uq, w_ukv, ssm_conv_w, ssm_conv_b, ssm_dt_bias, ssm_a_log, ssm_d, ssm_norm, w_br, w_o, ln1_g, ln1_b, w_up, ffn_conv_w, ffn_conv_b, w_down, ln2_g, ln2_b):
    b, seq, d = x.shape
    assert b == 1 and meta_tokens.shape[0] == N_META
    depth = w_in.shape[0]
    alpha = (2 * depth) ** 0.25
    l = BLOCK + seq
    h0 = jnp.concatenate([jnp.zeros((PAD_FRONT, d), x.dtype), meta_tokens.astype(x.dtype), x[0]], axis=0)

    pos = jnp.maximum(jnp.arange(l) - PAD_FRONT, 0).astype(f32)
    inv_freq = 1.0 / (ROPE_THETA ** (jnp.arange(0, ROPE_DIM, 2, dtype=f32) / ROPE_DIM))
    ang = pos[:, None] * inv_freq[None, :]
    cos, sin = jnp.cos(ang), jnp.sin(ang)
    cs = jnp.concatenate([cos, cos, sin, sin], axis=1)

    h, hb = _layer_norm(h0, ln_in_g, ln_in_b)
    for i in range(depth):
        p = (w_in[i], mla_q_norm[i], mla_kv_norm[i], w_uq[i], w_ukv[i], ssm_conv_w[i], ssm_conv_b[i],
             ssm_dt_bias[i], ssm_a_log[i], ssm_d[i], ssm_norm[i], w_br[i], w_o[i], ln1_g[i], ln1_b[i],
             w_up[i], ffn_conv_w[i], ffn_conv_b[i], w_down[i], ln2_g[i], ln2_b[i])
        h, hb = _layer(h, hb, cs, p, alpha)
    return h[BLOCK:][None]
```

```python
import functools
import math

import jax
import jax.numpy as jnp
from jax import lax
from jax.experimental import pallas as pl
from jax.experimental.pallas import tpu as pltpu

f32 = jnp.float32
bf16 = jnp.bfloat16

BLOCK = 128
N_META = 16
PAD_FRONT = BLOCK - N_META
HEAD_DIM = 128
ROPE_DIM = 64
SSM_HEAD_DIM = 64
SSM_GROUPS = 4
SSM_STATE = 128
ROPE_THETA = 10000.0
LN_EPS = 1e-5
RMS_EPS = 1e-6
NEG_INF = -1e30
ATT_Q_TILE = 512
ATT_K_TILE = 256
MLA_K_TILE = 256
LOG2E = 1.4426950408889634

V7X_VMEM_BYTES = 64 * 1024 * 1024
VMEM_INTERNAL_SCRATCH = 8 * 1024 * 1024


def _vmem_limit(block_bytes):
    return int(min(2 * block_bytes + VMEM_INTERNAL_SCRATCH, V7X_VMEM_BYTES - 4 * 1024 * 1024))


def _pick(n, cands):
    for c in cands:
        if n % c == 0:
            return c
    raise ValueError(f"no tile for {n}")


def _dot(a, b):
    return jnp.dot(a, b, preferred_element_type=f32)


def _dot_nt(a, b):
    return lax.dot_general(a, b, (((1,), (1,)), ((), ())), preferred_element_type=f32)


def _split3(x):
    hi = x.astype(bf16)
    r = x - hi.astype(f32)
    mid = r.astype(bf16)
    lo = (r - mid.astype(f32)).astype(bf16)
    return hi, mid, lo


def _dot3_lhs(x, w):
    hi, mid, lo = _split3(x)
    return _dot(hi, w) + _dot(mid, w) + _dot(lo, w)


def _dot3_rhs(w, x):
    hi, mid, lo = _split3(x)
    return _dot(w, hi) + _dot(w, mid) + _dot(w, lo)


def _sigmoid(x):
    return 1.0 / (1.0 + jnp.exp(-x))


def _softplus(x):
    return jnp.maximum(x, 0.0) + jnp.log(1.0 + jnp.exp(-jnp.abs(x)))


def _mm_kernel(a_ref, w_ref, o_ref):
    o_ref[...] = _dot(a_ref[...], w_ref[...]).astype(o_ref.dtype)


def _matmul(a, w, out_dtype, name="mm"):
    m, k = a.shape
    n = w.shape[1]
    tm = _pick(m, (640, 512, 384, 256, 128))
    tn = _pick(n, tuple(t for t in (1024, 512, 256, 128) if k * t * 2 <= 8 * 1024 * 1024))
    blk = tm * k * 2 + k * tn * 2 + tm * tn * jnp.dtype(out_dtype).itemsize
    return pl.pallas_call(
        _mm_kernel, grid=(n // tn, m // tm),
        in_specs=[pl.BlockSpec((tm, k), lambda j, i: (i, 0)),
                  pl.BlockSpec((k, tn), lambda j, i: (0, j))],
        out_specs=pl.BlockSpec((tm, tn), lambda j, i: (i, j)),
        out_shape=jax.ShapeDtypeStruct((m, n), out_dtype),
        compiler_params=pltpu.CompilerParams(
            dimension_semantics=("parallel", "parallel"), vmem_limit_bytes=_vmem_limit(blk)),
        name=name,
    )(a, w)


def _mm_w32_kernel(a_ref, w_ref, *rest, alpha):
    if len(rest) == 3:
        r_ref, o_ref, wb_ref = rest
    else:
        r_ref = None
        o_ref, wb_ref = rest

    @pl.when(pl.program_id(1) == 0)
    def _():
        wb_ref[...] = w_ref[...].astype(bf16)

    acc = _dot(a_ref[...], wb_ref[...])
    if r_ref is not None:
        acc = alpha * r_ref[...] + acc
    o_ref[...] = acc.astype(o_ref.dtype)


def _matmul_w32(a, w_stack, layer, col0, n, out_dtype, res=None, alpha=1.0, name="mm32"):
    m, k = a.shape
    assert w_stack.shape[1] == k
    tm = _pick(m, (640, 512, 384, 256, 128))
    tn = _pick(math.gcd(n, col0) if col0 else n,
               tuple(t for t in (512, 256, 128) if k * t * 4 <= 8 * 1024 * 1024))
    cb0 = col0 // tn
    in_specs = [pl.BlockSpec((tm, k), lambda j, i: (i, 0)),
                pl.BlockSpec((None, k, tn), lambda j, i: (layer, 0, cb0 + j))]
    args = [a, w_stack]
    blk = tm * k * 2 + k * tn * 4 + tm * tn * jnp.dtype(out_dtype).itemsize
    if res is not None:
        in_specs.append(pl.BlockSpec((tm, tn), lambda j, i: (i, j)))
        args.append(res)
        blk += tm * tn * 4
    return pl.pallas_call(
        functools.partial(_mm_w32_kernel, alpha=alpha), grid=(n // tn, m // tm), in_specs=in_specs,
        out_specs=pl.BlockSpec((tm, tn), lambda j, i: (i, j)),
        out_shape=jax.ShapeDtypeStruct((m, n), out_dtype),
        scratch_shapes=[pltpu.VMEM((k, tn), bf16)],
        compiler_params=pltpu.CompilerParams(
            dimension_semantics=("arbitrary", "arbitrary"),
            vmem_limit_bytes=_vmem_limit(blk) + k * tn * 2),
        name=name,
    )(*args)


def _ln_kernel(x_ref, g_ref, b_ref, o_ref, ob_ref):
    x = x_ref[...]
    mu = jnp.mean(x, axis=-1, keepdims=True)
    xc = x - mu
    var = jnp.mean(xc * xc, axis=-1, keepdims=True)
    y = xc * lax.rsqrt(var + LN_EPS) * g_ref[...] + b_ref[...]
    o_ref[...] = y
    ob_ref[...] = y.astype(bf16)


def _layer_norm(x, g, b):
    m, d = x.shape
    tr = _pick(m, (320, 256, 128))
    blk = tr * d * (4 + 4 + 2)
    return pl.pallas_call(
        _ln_kernel, grid=(m // tr,),
        in_specs=[pl.BlockSpec((tr, d), lambda i: (i, 0)),
                  pl.BlockSpec((1, d), lambda i: (0, 0)),
                  pl.BlockSpec((1, d), lambda i: (0, 0))],
        out_specs=[pl.BlockSpec((tr, d), lambda i: (i, 0)),
                   pl.BlockSpec((tr, d), lambda i: (i, 0))],
        out_shape=[jax.ShapeDtypeStruct((m, d), f32), jax.ShapeDtypeStruct((m, d), bf16)],
        compiler_params=pltpu.CompilerParams(
            dimension_semantics=("parallel",), vmem_limit_bytes=_vmem_limit(blk)),
        name="layer_norm",
    )(x, g.reshape(1, d), b.reshape(1, d))


def _tile_lanes(x, width):
    return x if width == BLOCK else jnp.concatenate([x] * (width // BLOCK), axis=1)


def _causal_mask(shape, mode, coff):
    if mode == "full":
        return None
    row = lax.broadcasted_iota(jnp.int32, shape, 0)
    col = lax.broadcasted_iota(jnp.int32, shape, 1)
    if mode == "valid":
        return col >= PAD_FRONT
    if mode.startswith("strict"):
        allowed = col + coff < row
    else:
        allowed = col + coff <= row
    if mode.endswith("valid"):
        allowed = allowed & (col >= PAD_FRONT)
    return allowed


def _key_blocks(x):
    l = x.shape[0]
    return x.reshape(l // BLOCK, BLOCK, -1, HEAD_DIM).transpose(2, 0, 3, 1)


def _load_kt(kt_ref, k0, ktile):
    kb = k0 // BLOCK
    return jnp.concatenate([kt_ref[kb + i] for i in range(ktile // BLOCK)], axis=1)


def _sb_step(q, kt, v, us, carry, mode, coff, scale2):
    zl, hi, lo, rs = _sb_logits(q, kt, mode, coff, scale2)
    return _sb_weights_pv(zl, hi, lo, v, us, carry, mode, coff), rs


def _sb_logits(q, kt, mode, coff, scale2):
    z = _dot(q, kt) * scale2
    nabs = lax.bitcast_convert_type(lax.bitcast_convert_type(z, jnp.uint32) | jnp.uint32(0x80000000), f32)
    sp = jnp.maximum(z, 0.0) + jnp.log(1.0 + jnp.exp2(nabs)) * LOG2E
    zl = z - sp
    allowed = _causal_mask(z.shape, mode, coff)
    if allowed is not None:
        sp = jnp.where(allowed, sp, 0.0)
    hi = sp.astype(bf16)
    lo = (sp - hi.astype(f32)).astype(bf16)
    return zl, hi, lo, jnp.sum(sp, axis=1, keepdims=True)


def _sb_weights_pv(zl, hi, lo, v, us, carry, mode, coff):
    later = _dot(hi, us) + _dot(lo, us)
    w = jnp.exp2((zl - later) - _tile_lanes(carry, zl.shape[1]))
    allowed = _causal_mask(zl.shape, mode, coff)
    if allowed is not None:
        w = jnp.where(allowed, w, 0.0)
    return _dot(w.astype(bf16), v)


def _sb_kernel(q_ref, k_ref, v_ref, us_ref, o_ref, acc_ref, carry_ref, zl_ref, hi_ref, lo_ref, rs_ref,
               *, scale2, n_tiles):
    tq, tk = ATT_Q_TILE, ATT_K_TILE
    assert tq == 2 * tk
    us = us_ref[...]
    us_meta = us_ref[0:BLOCK, 0:BLOCK]

    pv, _ = _sb_step(q_ref[0:BLOCK, :], k_ref[0], v_ref[0:BLOCK, :], us_meta,
                     jnp.zeros((BLOCK, BLOCK), f32), "strict_valid", 0, scale2)
    o_ref[0:BLOCK, :] = pv.astype(o_ref.dtype)

    def q_body(m, c):
        q0 = pl.multiple_of(BLOCK + m * tq, BLOCK)
        q = q_ref[pl.ds(q0, tq), :]
        acc_ref[...] = jnp.zeros(acc_ref.shape, f32)
        carry_ref[...] = jnp.zeros(carry_ref.shape, f32)

        def step(k0, ktile, u, mode, coff):
            cb = carry_ref[...]
            pv, rs = _sb_step(q, _load_kt(k_ref, k0, ktile), v_ref[pl.ds(k0, ktile), :], u, cb, mode, coff, scale2)
            acc_ref[...] += pv
            carry_ref[...] = cb + rs

        def key0(t):
            return pl.multiple_of(q0 - (t + 1) * tk, BLOCK)

        def logits(buf, t):
            zl, hi, lo, rs = _sb_logits(q, _load_kt(k_ref, key0(t), tk), "full", None, scale2)
            zl_ref[buf] = zl
            hi_ref[buf] = hi
            lo_ref[buf] = lo
            rs_ref[buf] = jnp.broadcast_to(rs, (tq, BLOCK))

        def weights(buf, t):
            cb = carry_ref[...]
            acc_ref[...] += _sb_weights_pv(zl_ref[buf], hi_ref[buf], lo_ref[buf], v_ref[pl.ds(key0(t), tk), :],
                                           us, cb, "full", None)
            carry_ref[...] = cb + rs_ref[buf]

        for d in reversed(range(tq // tk)):
            step(q0 + d * tk, tk, us, "strict", d * tk)

        @pl.when(m > 0)
        def _():
            logits(0, 0)

            def k_body(jj, c2):
                logits(1, 2 * jj + 1)
                weights(0, 2 * jj)
                logits(0, 2 * jj + 2)
                weights(1, 2 * jj + 1)
                return c2

            lax.fori_loop(0, m - 1, k_body, 0)
            logits(1, 2 * m - 1)
            weights(0, 2 * m - 2)
            weights(1, 2 * m - 1)

        step(0, BLOCK, us_meta, "valid", None)
        o_ref[pl.ds(q0, tq), :] = acc_ref[...].astype(o_ref.dtype)
        return c

    lax.fori_loop(0, n_tiles, q_body, 0)


def _sb_attention(qkv, heads):
    l = qkv.shape[0]
    n_tiles = (l - BLOCK) // ATT_Q_TILE
    assert BLOCK + n_tiles * ATT_Q_TILE == l
    idx = jnp.arange(ATT_K_TILE)
    us = (idx[:, None] > idx[None, :]).astype(bf16)
    blk = 4 * l * HEAD_DIM * 2 + ATT_K_TILE * ATT_K_TILE * 2
    kt = _key_blocks(qkv[:, heads * HEAD_DIM:2 * heads * HEAD_DIM])
    return pl.pallas_call(
        functools.partial(_sb_kernel, scale2=HEAD_DIM ** -0.5 * LOG2E, n_tiles=n_tiles),
        grid=(heads,),
        in_specs=[pl.BlockSpec((l, HEAD_DIM), lambda h: (0, h)),
                  pl.BlockSpec((None, l // BLOCK, HEAD_DIM, BLOCK), lambda h: (h, 0, 0, 0)),
                  pl.BlockSpec((l, HEAD_DIM), lambda h: (0, 2 * heads + h)),
                  pl.BlockSpec((ATT_K_TILE, ATT_K_TILE), lambda h: (0, 0))],
        out_specs=pl.BlockSpec((l, HEAD_DIM), lambda h: (0, h)),
        out_shape=jax.ShapeDtypeStruct((l, heads * HEAD_DIM), bf16),
        scratch_shapes=[pltpu.VMEM((ATT_Q_TILE, HEAD_DIM), f32), pltpu.VMEM((ATT_Q_TILE, BLOCK), f32),
                        pltpu.VMEM((2, ATT_Q_TILE, ATT_K_TILE), f32),
                        pltpu.VMEM((2, ATT_Q_TILE, ATT_K_TILE), bf16),
                        pltpu.VMEM((2, ATT_Q_TILE, ATT_K_TILE), bf16),
                        pltpu.VMEM((2, ATT_Q_TILE, BLOCK), f32)],
        compiler_params=pltpu.CompilerParams(
            dimension_semantics=("parallel",), vmem_limit_bytes=_vmem_limit(blk)),
        name="sb_attention",
    )(qkv, kt, qkv, us)


def _rope_mix(y2, cs):
    t = y2 * cs
    pe = t + pltpu.roll(t, ROPE_DIM, axis=1)
    lane = lax.broadcasted_iota(jnp.int32, pe.shape, 1)
    return jnp.where(lane < ROPE_DIM, pe, 0.0)


def _rms(x, g):
    ms = jnp.mean(x * x, axis=-1, keepdims=True)
    return x * lax.rsqrt(ms + RMS_EPS) * g


def _q_up_kernel(cq_ref, g_ref, w_ref, cs_ref, o_ref, xn_ref):
    @pl.when(pl.program_id(1) == 0)
    def _():
        xn_ref[...] = _rms(cq_ref[...], g_ref[...]).astype(bf16)

    y = _dot(xn_ref[...], w_ref[...])
    o_ref[:, 0:HEAD_DIM] = y[:, 0:HEAD_DIM].astype(bf16)
    o_ref[:, HEAD_DIM:] = _rope_mix(y[:, HEAD_DIM:], cs_ref[...]).astype(bf16)


def _kv_up_kernel(ckv_ref, g_ref, wk_ref, wv_ref, kpe_ref, cs_ref, k_ref, v_ref, xn_ref):
    @pl.when(pl.program_id(1) == 0)
    def _():
        xn_ref[...] = _rms(ckv_ref[...], g_ref[...]).astype(bf16)

    xn = xn_ref[...]
    k_ref[:, 0:HEAD_DIM] = _dot(xn, wk_ref[...]).astype(bf16)
    k_ref[:, HEAD_DIM:] = _rope_mix(kpe_ref[...], cs_ref[...]).astype(bf16)
    v_ref[:, 0:HEAD_DIM] = _dot(xn, wv_ref[...]).astype(bf16)
    v_ref[:, HEAD_DIM:] = jnp.ones((v_ref.shape[0], HEAD_DIM), bf16)


def _mla_softmax_pv(s, v1, m_old, mode, coff):
    allowed = _causal_mask(s.shape, mode, coff)
    if allowed is not None:
        s = jnp.where(allowed, s, NEG_INF)
    m_new = jnp.maximum(m_old, jnp.max(s, axis=1, keepdims=True))
    p = jnp.exp2(s - _tile_lanes(m_new, s.shape[1]))
    return m_new, jnp.exp2(m_old - m_new), _dot(p.astype(bf16), v1)


def _mla_kernel(q_ref, k_ref, v_ref, o_ref, m_ref, acc_ref, s_ref, *, scale2, n_tiles):
    tq, tk = ATT_Q_TILE, MLA_K_TILE
    assert tq == 2 * tk

    s = _dot(q_ref[0:BLOCK, :], k_ref[0]) * scale2
    _, _, pv = _mla_softmax_pv(s, v_ref[0:BLOCK, :], jnp.full((BLOCK, BLOCK), NEG_INF, f32), "causal_valid", 0)
    o_ref[0:BLOCK, :] = (pv[:, 0:HEAD_DIM] / pv[:, HEAD_DIM:]).astype(o_ref.dtype)

    def q_body(mi, c):
        q0 = pl.multiple_of(BLOCK + mi * tq, BLOCK)
        q = q_ref[pl.ds(q0, tq), :]

        def scores(t):
            return _dot(q, _load_kt(k_ref, pl.multiple_of(BLOCK + t * tk, BLOCK), tk)) * scale2

        def consume(s, v1, mode, coff):
            m_new, corr, pv = _mla_softmax_pv(s, v1, m_ref[...], mode, coff)
            acc_ref[...] = _tile_lanes(corr, 2 * HEAD_DIM) * acc_ref[...] + pv
            m_ref[...] = m_new

        def consume_tile(buf, t, mode, coff):
            consume(s_ref[buf], v_ref[pl.ds(pl.multiple_of(BLOCK + t * tk, BLOCK), tk), :], mode, coff)

        m_ref[...] = jnp.full(m_ref.shape, NEG_INF, f32)
        acc_ref[...] = jnp.zeros(acc_ref.shape, f32)
        consume(_dot(q, k_ref[0]) * scale2, v_ref[0:BLOCK, :], "valid", None)

        s_ref[0] = scores(0)

        def k_body(j, c2):
            s_ref[1] = scores(2 * j + 1)
            consume_tile(0, 2 * j, "full", None)
            s_ref[0] = scores(2 * j + 2)
            consume_tile(1, 2 * j + 1, "full", None)
            return c2

        lax.fori_loop(0, mi, k_body, 0)
        s_ref[1] = scores(2 * mi + 1)
        consume_tile(0, 2 * mi, "causal", 0)
        consume_tile(1, 2 * mi + 1, "causal", tk)
        acc = acc_ref[...]
        o_ref[pl.ds(q0, tq), :] = (acc[:, 0:HEAD_DIM] / acc[:, HEAD_DIM:]).astype(o_ref.dtype)
        return c

    lax.fori_loop(0, n_tiles, q_body, 0)


def _mla_branch(cqkv, kpe, cs, q_norm, kv_norm, w_q, w_k, w_v):
    l = cqkv.shape[0]
    heads, q_rank, _ = w_q.shape
    kv_rank = w_k.shape[1]
    assert q_rank % kv_rank == 0
    qk_dim = 2 * HEAD_DIM
    tm = _pick(l, (640, 512, 384, 256, 128))
    grid = (l // tm, heads)
    q_cat = pl.pallas_call(
        _q_up_kernel, grid=grid,
        in_specs=[pl.BlockSpec((tm, q_rank), lambda i, h: (i, 0)),
                  pl.BlockSpec((1, q_rank), lambda i, h: (0, 0)),
                  pl.BlockSpec((None, q_rank, qk_dim), lambda i, h: (h, 0, 0)),
                  pl.BlockSpec((tm, BLOCK), lambda i, h: (i, 0))],
        out_specs=pl.BlockSpec((None, tm, qk_dim), lambda i, h: (h, i, 0)),
        out_shape=jax.ShapeDtypeStruct((heads, l, qk_dim), bf16),
        scratch_shapes=[pltpu.VMEM((tm, q_rank), bf16)],
        compiler_params=pltpu.CompilerParams(
            dimension_semantics=("parallel", "arbitrary"),
            vmem_limit_bytes=_vmem_limit(tm * q_rank * 6 + q_rank * qk_dim * 2 + tm * qk_dim * 2 + tm * BLOCK * 4)),
        name="mla_q_up",
    )(cqkv, q_norm.reshape(1, q_rank), w_q, cs)
    k_cat, v_cat = pl.pallas_call(
        _kv_up_kernel, grid=grid,
        in_specs=[pl.BlockSpec((tm, kv_rank), lambda i, h: (i, q_rank // kv_rank)),
                  pl.BlockSpec((1, kv_rank), lambda i, h: (0, 0)),
                  pl.BlockSpec((None, kv_rank, HEAD_DIM), lambda i, h: (h, 0, 0)),
                  pl.BlockSpec((None, kv_rank, HEAD_DIM), lambda i, h: (h, 0, 0)),
                  pl.BlockSpec((tm, BLOCK), lambda i, h: (i, 0)),
                  pl.BlockSpec((tm, BLOCK), lambda i, h: (i, 0))],
        out_specs=[pl.BlockSpec((None, tm, qk_dim), lambda i, h: (h, i, 0)),
                   pl.BlockSpec((None, tm, 2 * HEAD_DIM), lambda i, h: (h, i, 0))],
        out_shape=[jax.ShapeDtypeStruct((heads, l, qk_dim), bf16),
                   jax.ShapeDtypeStruct((heads, l, 2 * HEAD_DIM), bf16)],
        scratch_shapes=[pltpu.VMEM((tm, kv_rank), bf16)],
        compiler_params=pltpu.CompilerParams(
            dimension_semantics=("parallel", "arbitrary"),
            vmem_limit_bytes=_vmem_limit(tm * kv_rank * 6 + 2 * kv_rank * HEAD_DIM * 2 + tm * BLOCK * 8
                                         + tm * (qk_dim + 2 * HEAD_DIM) * 2)),
        name="mla_kv_up",
    )(cqkv, kv_norm.reshape(1, kv_rank), w_k, w_v, kpe, cs)

    n_tiles = (l - BLOCK) // ATT_Q_TILE
    blk = 3 * l * qk_dim * 2 + l * HEAD_DIM * 2
    kt = k_cat.reshape(heads, l // BLOCK, BLOCK, qk_dim).transpose(0, 1, 3, 2)
    return pl.pallas_call(
        functools.partial(_mla_kernel, scale2=(HEAD_DIM + ROPE_DIM) ** -0.5 * LOG2E, n_tiles=n_tiles),
        grid=(heads,),
        in_specs=[pl.BlockSpec((None, l, qk_dim), lambda h: (h, 0, 0)),
                  pl.BlockSpec((None, l // BLOCK, qk_dim, BLOCK), lambda h: (h, 0, 0, 0)),
                  pl.BlockSpec((None, l, 2 * HEAD_DIM), lambda h: (h, 0, 0))],
        out_specs=pl.BlockSpec((l, HEAD_DIM), lambda h: (0, h)),
        out_shape=jax.ShapeDtypeStruct((l, heads * HEAD_DIM), bf16),
        scratch_shapes=[pltpu.VMEM((ATT_Q_TILE, BLOCK), f32), pltpu.VMEM((ATT_Q_TILE, 2 * HEAD_DIM), f32),
                        pltpu.VMEM((2, ATT_Q_TILE, MLA_K_TILE), f32)],
        compiler_params=pltpu.CompilerParams(
            dimension_semantics=("parallel",), vmem_limit_bytes=_vmem_limit(blk)),
        name="mla_attention",
    )(q_cat, kt, v_cat)


def _ssd_kernel(xbc_ref, xprev_ref, z_ref, dtr_ref, dtrt_ref, cw_ref, cb_ref, dtb_ref, dtbt_ref,
                alog_ref, alogt_ref, dexp_ref, ng_ref, e64_ref, e128_ref, tril_ref, triu_ref,
                o_ref, ht_ref, xs_ref, bc_ref, xdt_ref, xdec_ref, y_ref, *, d_inner, heads, conv_k):
    c = pl.program_id(0)
    n = SSM_STATE
    pairs = heads // 2
    pairs_per_group = pairs // SSM_GROUPS
    gn = SSM_GROUPS * n

    @pl.when(c == 0)
    def _():
        ht_ref[...] = jnp.zeros(ht_ref.shape, f32)

    row = lax.broadcasted_iota(jnp.int32, (BLOCK, 1), 0)
    grow = c * BLOCK + row
    valid_r = grow >= PAD_FRONT

    x = jnp.where(valid_r, xbc_ref[...], 0.0)
    xp = jnp.where(grow - BLOCK >= PAD_FRONT, xprev_ref[...], 0.0)
    acc = x * cw_ref[conv_k - 1:conv_k, :] + cb_ref[...]
    for s in range(1, conv_k):
        shifted = jnp.where(row >= s, pltpu.roll(x, s, axis=0), pltpu.roll(xp, s, axis=0))
        acc = acc + shifted * cw_ref[conv_k - 1 - s:conv_k - s, :]
    xc = acc * _sigmoid(acc)
    xs_ref[...] = xc[:, 0:d_inner]
    bc_ref[...] = xc[:, d_inner:]

    dt = jnp.where(valid_r, _softplus(dtr_ref[...] + dtb_ref[...]), 0.0)
    adt = -jnp.exp(alog_ref[...]) * dt
    a_cum = _dot3_rhs(tril_ref[...], adt)
    col = lax.broadcasted_iota(jnp.int32, (1, BLOCK), 1)
    dtt = jnp.where(c * BLOCK + col >= PAD_FRONT, _softplus(dtrt_ref[...] + dtbt_ref[...]), 0.0)
    a_cum_t = _dot3_lhs(-jnp.exp(alogt_ref[...]) * dtt, triu_ref[...])

    e64 = e64_ref[...]
    dt64 = _dot3_lhs(dt, e64)
    ac64 = _dot3_lhs(a_cum, e64)
    ac128 = _dot3_lhs(a_cum, e128_ref[...])
    a_last = ac64[BLOCK - 1:BLOCK, :]
    xdt = xs_ref[...] * dt64
    xdt_ref[...] = xdt.astype(bf16)
    xdec_ref[...] = (xdt * jnp.exp(a_last - ac64)).astype(bf16)
    eac = jnp.exp(ac64)
    chunk_decay = jnp.exp(a_last)

    li = lax.broadcasted_iota(jnp.int32, (BLOCK, BLOCK), 0)
    si = lax.broadcasted_iota(jnp.int32, (BLOCK, BLOCK), 1)
    causal = li >= si
    low_half = si < SSM_HEAD_DIM
    for g in range(SSM_GROUPS):
        bg = bc_ref[:, g * n:(g + 1) * n]
        cg = bc_ref[:, gn + g * n:gn + (g + 1) * n].astype(bf16)
        bgt = bg.T.astype(bf16)
        cb = _dot(cg, bgt)
        for jp in range(pairs_per_group):
            j = g * pairs_per_group + jp
            lanes = slice(j * BLOCK, (j + 1) * BLOCK)
            xpair = xdt_ref[:, lanes]
            ys = []
            for hh in (2 * j, 2 * j + 1):
                seg = ac128[:, hh * BLOCK:(hh + 1) * BLOCK] - a_cum_t[hh:hh + 1, :]
                lm = jnp.exp(jnp.where(causal, seg, NEG_INF))
                ys.append(_dot((cb * lm).astype(bf16), xpair))
            y_diag = jnp.where(low_half, ys[0], ys[1])
            ht = ht_ref[j]
            y_off = _dot(cg, ht.astype(bf16)) * eac[:, lanes]
            ht_ref[j] = ht * chunk_decay[:, lanes] + _dot(bgt, xdec_ref[:, lanes])
            y_ref[:, lanes] = y_diag + y_off + xs_ref[:, lanes] * dexp_ref[:, lanes]

    zz = z_ref[...]
    y = y_ref[...] * (zz * _sigmoid(zz))
    gs = d_inner // SSM_GROUPS
    for g in range(SSM_GROUPS):
        yg = y[:, g * gs:(g + 1) * gs]
        o_ref[:, g * gs:(g + 1) * gs] = _rms(yg, ng_ref[:, g * gs:(g + 1) * gs]).astype(o_ref.dtype)


def _ssd_branch(z, xbc, dtr, dtr_t, conv_w, conv_b, dt_bias, a_log, d_skip, norm_g):
    l, d_inner = z.shape
    d_xbc = xbc.shape[1]
    heads = dt_bias.shape[0]
    conv_k = conv_w.shape[0]
    assert heads <= BLOCK and heads % (2 * SSM_GROUPS) == 0 and d_inner == heads * SSM_HEAD_DIM
    assert d_xbc == d_inner + 2 * SSM_GROUPS * SSM_STATE
    nc = l // BLOCK
    pad = BLOCK - heads
    dtb = jnp.pad(dt_bias, (0, pad))
    alog = jnp.pad(a_log, (0, pad))
    hid = jnp.arange(BLOCK)
    e64 = ((jnp.arange(d_inner)[None, :] // SSM_HEAD_DIM) == hid[:, None]).astype(bf16)
    e128 = ((jnp.arange(heads * BLOCK)[None, :] // BLOCK) == hid[:, None]).astype(bf16)
    tril = (hid[:, None] >= hid[None, :]).astype(bf16)
    triu = (hid[:, None] <= hid[None, :]).astype(bf16)
    full = lambda shape: pl.BlockSpec(shape, lambda c: (0,) * len(shape))
    blk = (2 * BLOCK * d_xbc * 4 + BLOCK * d_inner * 4 + BLOCK * d_inner * 2 + 4 * BLOCK * BLOCK * 4
           + conv_k * d_xbc * 4 + BLOCK * (d_inner + heads * BLOCK) * 2)
    scratch = [pltpu.VMEM((heads // 2, SSM_STATE, BLOCK), f32),
               pltpu.VMEM((BLOCK, d_inner), f32),
               pltpu.VMEM((BLOCK, d_xbc - d_inner), f32),
               pltpu.VMEM((BLOCK, d_inner), bf16),
               pltpu.VMEM((BLOCK, d_inner), bf16),
               pltpu.VMEM((BLOCK, d_inner), f32)]
    scratch_bytes = (heads // 2) * SSM_STATE * BLOCK * 4 + BLOCK * (d_inner * 12 + (d_xbc - d_inner) * 4)
    return pl.pallas_call(
        functools.partial(_ssd_kernel, d_inner=d_inner, heads=heads, conv_k=conv_k),
        grid=(nc,),
        in_specs=[pl.BlockSpec((BLOCK, d_xbc), lambda c: (c, 0)),
                  pl.BlockSpec((BLOCK, d_xbc), lambda c: (jnp.maximum(c - 1, 0), 0)),
                  pl.BlockSpec((BLOCK, d_inner), lambda c: (c, 0)),
                  pl.BlockSpec((BLOCK, BLOCK), lambda c: (c, 0)),
                  pl.BlockSpec((BLOCK, BLOCK), lambda c: (0, c)),
                  full((conv_k, d_xbc)), full((1, d_xbc)),
                  full((1, BLOCK)), full((BLOCK, 1)), full((1, BLOCK)), full((BLOCK, 1)),
                  full((1, d_inner)), full((1, d_inner)),
                  full((BLOCK, d_inner)), full((BLOCK, heads * BLOCK)),
                  full((BLOCK, BLOCK)), full((BLOCK, BLOCK))],
        out_specs=pl.BlockSpec((BLOCK, d_inner), lambda c: (c, 0)),
        out_shape=jax.ShapeDtypeStruct((l, d_inner), bf16),
        scratch_shapes=scratch,
        compiler_params=pltpu.CompilerParams(
            dimension_semantics=("arbitrary",), vmem_limit_bytes=_vmem_limit(blk + scratch_bytes)),
        name="ssd",
    )(xbc, xbc, z, dtr, dtr_t, conv_w, conv_b.reshape(1, d_xbc),
      dtb.reshape(1, BLOCK), dtb.reshape(BLOCK, 1), alog.reshape(1, BLOCK), alog.reshape(BLOCK, 1),
      jnp.repeat(d_skip, SSM_HEAD_DIM).reshape(1, d_inner), norm_g.reshape(1, d_inner),
      e64, e128, tril, triu)


def _merge_kernel(oa_ref, ob_ref, oc_ref, w_ref, ga_ref, gb_ref, gc_ref, o_ref):
    acc = _sigmoid(ga_ref[...]) * _dot(oa_ref[...], w_ref[0])
    acc = acc + _sigmoid(gb_ref[...]) * _dot(ob_ref[...], w_ref[1])
    acc = acc + _sigmoid(gc_ref[...]) * _dot(oc_ref[...], w_ref[2])
    o_ref[...] = acc.astype(o_ref.dtype)


def _merge(o_a, o_b, o_c, w_br, gates):
    l, bw = o_a.shape
    d = w_br.shape[2]
    tm = _pick(l, (640, 512, 384, 256, 128))
    tn = _pick(d, (512, 256, 128))
    nb = d // tn
    o_spec = pl.BlockSpec((tm, bw), lambda i, j: (i, 0))
    g_spec = lambda b: pl.BlockSpec((tm, tn), lambda i, j: (i, b * nb + j))
    blk = 3 * tm * bw * 2 + 3 * bw * tn * 2 + 3 * tm * tn * 4 + tm * tn * 2
    return pl.pallas_call(
        _merge_kernel, grid=(l // tm, nb),
        in_specs=[o_spec, o_spec, o_spec,
                  pl.BlockSpec((3, bw, tn), lambda i, j: (0, 0, j)),
                  g_spec(0), g_spec(1), g_spec(2)],
        out_specs=pl.BlockSpec((tm, tn), lambda i, j: (i, j)),
        out_shape=jax.ShapeDtypeStruct((l, d), bf16),
        compiler_params=pltpu.CompilerParams(
            dimension_semantics=("parallel", "parallel"), vmem_limit_bytes=_vmem_limit(blk)),
        name="merge",
    )(o_a, o_b, o_c, w_br, gates, gates, gates)


def _shift_rows(x, halo, s):
    r = pltpu.roll(x, s, axis=0)
    hr = pltpu.roll(halo, s, axis=0)
    row8 = lax.broadcasted_iota(jnp.int32, halo.shape, 0)
    top = jnp.where(row8 < s, hr, r[0:8, :])
    return jnp.concatenate([top, r[8:, :]], axis=0)


def _ffn_up_glu_kernel(a_ref, wg_ref, wv_ref, cwg_ref, cwv_ref, bg_ref, bv_ref, o_ref,
                       wgb_ref, wvb_ref, tail_g_ref, tail_v_ref, *, conv_k, tm):
    i = pl.program_id(1)

    @pl.when(i == 0)
    def _():
        wgb_ref[...] = wg_ref[...].astype(bf16)
        wvb_ref[...] = wv_ref[...].astype(bf16)
        tail_g_ref[...] = jnp.zeros(tail_g_ref.shape, f32)
        tail_v_ref[...] = jnp.zeros(tail_v_ref.shape, f32)

    a = a_ref[...]
    row = lax.broadcasted_iota(jnp.int32, (tm, 1), 0) + i * tm

    def conv(wb_ref, tail_ref, cw_ref, b_ref):
        u = jnp.where(row >= PAD_FRONT, _dot(a, wb_ref[...]), 0.0)
        halo = tail_ref[...]
        acc = u * cw_ref[conv_k - 1:conv_k, :] + b_ref[...]
        for s in range(1, conv_k):
            acc = acc + _shift_rows(u, halo, s) * cw_ref[conv_k - 1 - s:conv_k - s, :]
        tail_ref[...] = u[tm - 8:tm, :]
        return acc

    gate = conv(wgb_ref, tail_g_ref, cwg_ref, bg_ref)
    val = conv(wvb_ref, tail_v_ref, cwv_ref, bv_ref)
    o_ref[...] = (gate * _sigmoid(gate) * val).astype(o_ref.dtype)


def _ffn_up_glu(a, w_up_stack, layer, conv_w, conv_b):
    l, k = a.shape
    two_ff = w_up_stack.shape[2]
    d_ff = two_ff // 2
    conv_k = conv_w.shape[0]
    assert conv_k <= 8
    tm = _pick(l, (640, 512, 384, 256, 128))
    tc = _pick(d_ff, (256, 128))
    nb = d_ff // tc
    cb = conv_b.reshape(1, two_ff)
    blk = tm * k * 2 + 2 * k * tc * 4 + tm * tc * 2 + 2 * (conv_k + 1) * tc * 4
    scratch_bytes = 2 * k * tc * 2 + 2 * 8 * tc * 4
    return pl.pallas_call(
        functools.partial(_ffn_up_glu_kernel, conv_k=conv_k, tm=tm),
        grid=(nb, l // tm),
        in_specs=[pl.BlockSpec((tm, k), lambda j, i: (i, 0)),
                  pl.BlockSpec((None, k, tc), lambda j, i: (layer, 0, j)),
                  pl.BlockSpec((None, k, tc), lambda j, i: (layer, 0, nb + j)),
                  pl.BlockSpec((conv_k, tc), lambda j, i: (0, j)),
                  pl.BlockSpec((conv_k, tc), lambda j, i: (0, nb + j)),
                  pl.BlockSpec((1, tc), lambda j, i: (0, j)),
                  pl.BlockSpec((1, tc), lambda j, i: (0, nb + j))],
        out_specs=pl.BlockSpec((tm, tc), lambda j, i: (i, j)),
        out_shape=jax.ShapeDtypeStruct((l, d_ff), bf16),
        scratch_shapes=[pltpu.VMEM((k, tc), bf16), pltpu.VMEM((k, tc), bf16),
                        pltpu.VMEM((8, tc), f32), pltpu.VMEM((8, tc), f32)],
        compiler_params=pltpu.CompilerParams(
            dimension_semantics=("arbitrary", "arbitrary"),
            vmem_limit_bytes=_vmem_limit(blk) + scratch_bytes),
        name="ffn_up_glu",
    )(a, w_up_stack, w_up_stack, conv_w, conv_w, cb, cb)


def _rotate_half_cols(w):
    half = w.shape[-1] // 2
    return jnp.concatenate([-w[..., half:], w[..., :half]], axis=-1)


def _layer(h, hb, cs, p, alpha, stacks, layer):
    w_in_s, w_o_s, w_up_s, w_down_s = stacks
    (w_in, q_norm, kv_norm, w_uq, w_ukv, conv_w, conv_b, dt_bias, a_log, d_skip, ssm_norm, w_br,
     ln1_g, ln1_b, f_conv_w, f_conv_b, ln2_g, ln2_b) = p
    d = h.shape[1]
    bw = w_br.shape[1]
    sb_heads = bw // HEAD_DIM
    q_rank, kv_rank = q_norm.shape[0], kv_norm.shape[0]
    ssm_heads = dt_bias.shape[0]
    d_inner = ssm_norm.shape[0]
    d_xbc = conv_w.shape[1]
    mla_heads = w_uq.shape[1] // (HEAD_DIM + ROPE_DIM)

    sizes = (bw, bw, bw, q_rank, kv_rank, ROPE_DIM, d_inner, d_xbc, ssm_heads, 3 * d)
    offs = [0]
    for s in sizes:
        offs.append(offs[-1] + s)
    assert offs[-1] == w_in.shape[1]
    seg = lambda i: w_in[:, offs[i]:offs[i + 1]]
    w_kpe = seg(5)
    w_misc = jnp.concatenate([w_kpe, _rotate_half_cols(w_kpe), seg(8),
                              jnp.zeros((d, BLOCK - ssm_heads), f32)], axis=1)
    qkv = _matmul_w32(hb, w_in_s, layer, 0, offs[3], bf16, name="in_qkv")
    cqkv = _matmul_w32(hb, w_in_s, layer, offs[3], offs[5] - offs[3], f32, name="in_cqkv")
    misc = _matmul(hb, w_misc.astype(bf16), f32, name="in_misc")
    z = _matmul(hb, seg(6).astype(bf16), f32, name="in_z")
    xbc = _matmul(hb, seg(7).astype(bf16), f32, name="in_xbc")
    gates = _matmul(hb, seg(9).astype(bf16), f32, name="in_gates")

    o_a = _sb_attention(qkv, sb_heads)

    wq = w_uq.reshape(q_rank, mla_heads, HEAD_DIM + ROPE_DIM)
    wq_pe = wq[..., HEAD_DIM:]
    wq = jnp.concatenate([wq[..., :HEAD_DIM], wq_pe, _rotate_half_cols(wq_pe)], axis=-1)
    wq = wq.transpose(1, 0, 2).astype(bf16)
    wkv = w_ukv.reshape(kv_rank, mla_heads, 2 * HEAD_DIM).transpose(1, 0, 2).astype(bf16)
    o_b = _mla_branch(cqkv, misc[:, :BLOCK], cs, q_norm, kv_norm, wq, wkv[..., :HEAD_DIM], wkv[..., HEAD_DIM:])

    dtr = misc[:, BLOCK:]
    o_c = _ssd_branch(z, xbc, dtr, dtr.T, conv_w, conv_b, dt_bias, a_log, d_skip, ssm_norm)

    merged = _merge(o_a, o_b, o_c, w_br.astype(bf16), gates)
    s1 = _matmul_w32(merged, w_o_s, layer, 0, d, f32, res=h, alpha=alpha, name="out_proj")
    h, hb = _layer_norm(s1, ln1_g, ln1_b)

    act = _ffn_up_glu(hb, w_up_s, layer, f_conv_w, f_conv_b)
    s2 = _matmul_w32(act, w_down_s, layer, 0, d, f32, res=h, alpha=alpha, name="ffn_down")
    return _layer_norm(s2, ln2_g, ln2_b)


def kernel(x, meta_tokens, ln_in_g, ln_in_b, w_in, mla_q_norm, mla_kv_norm, w_uq, w_ukv, ssm_conv_w, ssm_conv_b, ssm_dt_bias, ssm_a_log, ssm_d, ssm_norm, w_br, w_o, ln1_g, ln1_b, w_up, ffn_conv_w, ffn_conv_b, w_down, ln2_g, ln2_b):
    b, seq, d = x.shape
    assert b == 1 and meta_tokens.shape[0] == N_META
    depth = w_in.shape[0]
    alpha = (2 * depth) ** 0.25
    l = BLOCK + seq
    h0 = jnp.concatenate([jnp.zeros((PAD_FRONT, d), x.dtype), meta_tokens.astype(x.dtype), x[0]], axis=0)

    pos = jnp.maximum(jnp.arange(l) - PAD_FRONT, 0).astype(f32)
    inv_freq = 1.0 / (ROPE_THETA ** (jnp.arange(0, ROPE_DIM, 2, dtype=f32) / ROPE_DIM))
    ang = pos[:, None] * inv_freq[None, :]
    cos, sin = jnp.cos(ang), jnp.sin(ang)
    cs = jnp.concatenate([cos, cos, sin, sin], axis=1)

    h, hb = _layer_norm(h0, ln_in_g, ln_in_b)
    for i in range(depth):
        p = (w_in[i], mla_q_norm[i], mla_kv_norm[i], w_uq[i], w_ukv[i], ssm_conv_w[i], ssm_conv_b[i],
             ssm_dt_bias[i], ssm_a_log[i], ssm_d[i], ssm_norm[i], w_br[i], ln1_g[i], ln1_b[i],
             ffn_conv_w[i], ffn_conv_b[i], ln2_g[i], ln2_b[i])
        h, hb = _layer(h, hb, cs, p, alpha, (w_in, w_o, w_up, w_down), i)
    return h[BLOCK:][None]
```

```python
import functools
import math

import jax
import jax.numpy as jnp
from jax import lax
from jax.experimental import pallas as pl
from jax.experimental.pallas import tpu as pltpu

f32 = jnp.float32
bf16 = jnp.bfloat16

BLOCK = 128
N_META = 16
PAD_FRONT = BLOCK - N_META
HEAD_DIM = 128
ROPE_DIM = 64
SSM_HEAD_DIM = 64
SSM_GROUPS = 4
SSM_STATE = 128
ROPE_THETA = 10000.0
LN_EPS = 1e-5
RMS_EPS = 1e-6
NEG_INF = -1e30
ATT_Q_TILE = 512
ATT_K_TILE = 256
MLA_K_TILE = 256
LOG2E = 1.4426950408889634

V7X_VMEM_BYTES = 64 * 1024 * 1024
VMEM_INTERNAL_SCRATCH = 8 * 1024 * 1024


def _vmem_limit(block_bytes):
    return int(min(2 * block_bytes + VMEM_INTERNAL_SCRATCH, V7X_VMEM_BYTES - 4 * 1024 * 1024))


def _pick(n, cands):
    for c in cands:
        if n % c == 0:
            return c
    raise ValueError(f"no tile for {n}")


def _dot(a, b):
    return jnp.dot(a, b, preferred_element_type=f32)


def _dot_nt(a, b):
    return lax.dot_general(a, b, (((1,), (1,)), ((), ())), preferred_element_type=f32)


def _split3(x):
    hi = x.astype(bf16)
    r = x - hi.astype(f32)
    mid = r.astype(bf16)
    lo = (r - mid.astype(f32)).astype(bf16)
    return hi, mid, lo


def _dot3_lhs(x, w):
    hi, mid, lo = _split3(x)
    return _dot(hi, w) + _dot(mid, w) + _dot(lo, w)


def _dot3_rhs(w, x):
    hi, mid, lo = _split3(x)
    return _dot(w, hi) + _dot(w, mid) + _dot(w, lo)


def _sigmoid(x):
    return 1.0 / (1.0 + jnp.exp(-x))


def _softplus(x):
    return jnp.maximum(x, 0.0) + jnp.log(1.0 + jnp.exp(-jnp.abs(x)))


def _mm_kernel(a_ref, w_ref, o_ref):
    o_ref[...] = _dot(a_ref[...], w_ref[...]).astype(o_ref.dtype)


def _matmul(a, w, out_dtype, name="mm"):
    m, k = a.shape
    n = w.shape[1]
    tm = _pick(m, (640, 512, 384, 256, 128))
    tn = _pick(n, tuple(t for t in (1024, 512, 256, 128) if k * t * 2 <= 8 * 1024 * 1024))
    blk = tm * k * 2 + k * tn * 2 + tm * tn * jnp.dtype(out_dtype).itemsize
    return pl.pallas_call(
        _mm_kernel, grid=(n // tn, m // tm),
        in_specs=[pl.BlockSpec((tm, k), lambda j, i: (i, 0)),
                  pl.BlockSpec((k, tn), lambda j, i: (0, j))],
        out_specs=pl.BlockSpec((tm, tn), lambda j, i: (i, j)),
        out_shape=jax.ShapeDtypeStruct((m, n), out_dtype),
        compiler_params=pltpu.CompilerParams(
            dimension_semantics=("parallel", "parallel"), vmem_limit_bytes=_vmem_limit(blk)),
        name=name,
    )(a, w)


def _mm_w32_kernel(a_ref, w_ref, *rest, alpha):
    if len(rest) == 3:
        r_ref, o_ref, wb_ref = rest
    else:
        r_ref = None
        o_ref, wb_ref = rest

    @pl.when(pl.program_id(1) == 0)
    def _():
        wb_ref[...] = w_ref[...].astype(bf16)

    acc = _dot(a_ref[...], wb_ref[...])
    if r_ref is not None:
        acc = alpha * r_ref[...] + acc
    o_ref[...] = acc.astype(o_ref.dtype)


def _matmul_w32(a, w_stack, layer, col0, n, out_dtype, res=None, alpha=1.0, ragged=False, name="mm32"):
    m, k = a.shape
    assert w_stack.shape[1] == k
    tm = _pick(m, (640, 512, 384, 256, 128))
    tn = _pick(col0, (512, 256, 128)) if ragged else _pick(math.gcd(n, col0) if col0 else n, (512, 256, 128))
    cb0 = col0 // tn
    in_specs = [pl.BlockSpec((tm, k), lambda j, i: (i, 0)),
                pl.BlockSpec((None, k, tn), lambda j, i: (layer, 0, cb0 + j), pipeline_mode=pl.Buffered(1))]
    args = [a, w_stack]
    blk = tm * k * 2 + tm * tn * jnp.dtype(out_dtype).itemsize
    if res is not None:
        in_specs.append(pl.BlockSpec((tm, tn), lambda j, i: (i, j)))
        args.append(res)
        blk += tm * tn * 4
    return pl.pallas_call(
        functools.partial(_mm_w32_kernel, alpha=alpha), grid=(pl.cdiv(n, tn), m // tm), in_specs=in_specs,
        out_specs=pl.BlockSpec((tm, tn), lambda j, i: (i, j)),
        out_shape=jax.ShapeDtypeStruct((m, n), out_dtype),
        scratch_shapes=[pltpu.VMEM((k, tn), bf16)],
        compiler_params=pltpu.CompilerParams(
            dimension_semantics=("arbitrary", "arbitrary"),
            vmem_limit_bytes=_vmem_limit(blk) + k * tn * (4 + 2)),
        name=name,
    )(*args)


def _ln_kernel(x_ref, g_ref, b_ref, o_ref, ob_ref):
    x = x_ref[...]
    mu = jnp.mean(x, axis=-1, keepdims=True)
    xc = x - mu
    var = jnp.mean(xc * xc, axis=-1, keepdims=True)
    y = xc * lax.rsqrt(var + LN_EPS) * g_ref[...] + b_ref[...]
    o_ref[...] = y
    ob_ref[...] = y.astype(bf16)


def _layer_norm(x, g, b):
    m, d = x.shape
    tr = _pick(m, (320, 256, 128))
    blk = tr * d * (4 + 4 + 2)
    return pl.pallas_call(
        _ln_kernel, grid=(m // tr,),
        in_specs=[pl.BlockSpec((tr, d), lambda i: (i, 0)),
                  pl.BlockSpec((1, d), lambda i: (0, 0)),
                  pl.BlockSpec((1, d), lambda i: (0, 0))],
        out_specs=[pl.BlockSpec((tr, d), lambda i: (i, 0)),
                   pl.BlockSpec((tr, d), lambda i: (i, 0))],
        out_shape=[jax.ShapeDtypeStruct((m, d), f32), jax.ShapeDtypeStruct((m, d), bf16)],
        compiler_params=pltpu.CompilerParams(
            dimension_semantics=("parallel",), vmem_limit_bytes=_vmem_limit(blk)),
        name="layer_norm",
    )(x, g.reshape(1, d), b.reshape(1, d))


def _tile_lanes(x, width):
    return x if width == BLOCK else jnp.concatenate([x] * (width // BLOCK), axis=1)


def _causal_mask(shape, mode, coff):
    if mode == "full":
        return None
    row = lax.broadcasted_iota(jnp.int32, shape, 0)
    col = lax.broadcasted_iota(jnp.int32, shape, 1)
    if mode == "valid":
        return col >= PAD_FRONT
    if mode.startswith("strict"):
        allowed = col + coff < row
    else:
        allowed = col + coff <= row
    if mode.endswith("valid"):
        allowed = allowed & (col >= PAD_FRONT)
    return allowed


def _key_blocks(x):
    l = x.shape[0]
    return x.reshape(l // BLOCK, BLOCK, -1, HEAD_DIM).transpose(2, 0, 3, 1)


def _load_kt(kt_ref, k0, ktile):
    kb = k0 // BLOCK
    return jnp.concatenate([kt_ref[kb + i] for i in range(ktile // BLOCK)], axis=1)


def _sb_step(q, kt, v, us, carry, mode, coff, scale2):
    zl, hi, lo, rs = _sb_softplus(_dot(q, kt) * scale2, mode, coff)
    return _sb_weights_pv(zl, hi, lo, v, us, carry, mode, coff), rs


def _sb_softplus(z, mode, coff):
    nabs = lax.bitcast_convert_type(lax.bitcast_convert_type(z, jnp.uint32) | jnp.uint32(0x80000000), f32)
    sp = jnp.maximum(z, 0.0) + jnp.log(1.0 + jnp.exp2(nabs)) * LOG2E
    zl = z - sp
    allowed = _causal_mask(z.shape, mode, coff)
    if allowed is not None:
        sp = jnp.where(allowed, sp, 0.0)
    hi = sp.astype(bf16)
    lo = (sp - hi.astype(f32)).astype(bf16)
    return zl, hi, lo, jnp.sum(sp, axis=1, keepdims=True)


def _sb_weights_pv(zl, hi, lo, v, us, carry, mode, coff):
    later = _dot(hi, us) + _dot(lo, us)
    w = jnp.exp2((zl - later) - _tile_lanes(carry, zl.shape[1]))
    allowed = _causal_mask(zl.shape, mode, coff)
    if allowed is not None:
        w = jnp.where(allowed, w, 0.0)
    return _dot(w.astype(bf16), v)


def _sb_kernel(q_ref, k_ref, v_ref, us_ref, o_ref, acc_ref, carry_ref, z_ref, zl_ref, later_ref, rs_ref,
               *, scale2, n_tiles):
    tq, tk = ATT_Q_TILE, ATT_K_TILE
    assert tq == 2 * tk
    us = us_ref[...]
    us_meta = us_ref[0:BLOCK, 0:BLOCK]

    pv, _ = _sb_step(q_ref[0:BLOCK, :], k_ref[0], v_ref[0:BLOCK, :], us_meta,
                     jnp.zeros((BLOCK, BLOCK), f32), "strict_valid", 0, scale2)
    o_ref[0:BLOCK, :] = pv.astype(o_ref.dtype)

    def q_body(m, c):
        q0 = pl.multiple_of(BLOCK + m * tq, BLOCK)
        q = q_ref[pl.ds(q0, tq), :]
        acc_ref[...] = jnp.zeros(acc_ref.shape, f32)
        carry_ref[...] = jnp.zeros(carry_ref.shape, f32)

        def step(k0, ktile, u, mode, coff):
            cb = carry_ref[...]
            pv, rs = _sb_step(q, _load_kt(k_ref, k0, ktile), v_ref[pl.ds(k0, ktile), :], u, cb, mode, coff, scale2)
            acc_ref[...] += pv
            carry_ref[...] = cb + rs

        def key0(t):
            return pl.multiple_of(q0 - (t + 1) * tk, BLOCK)

        def scores(p, t):
            z_ref[p] = _dot(q, _load_kt(k_ref, key0(t), tk)) * scale2

        def sums(p):
            zl, hi, lo, rs = _sb_softplus(z_ref[p], "full", None)
            zl_ref[p] = zl
            rs_ref[p] = jnp.broadcast_to(rs, (tq, BLOCK))
            later_ref[p] = _dot(hi, us) + _dot(lo, us)

        def values(p, t):
            cb = carry_ref[...]
            w = jnp.exp2((zl_ref[p] - later_ref[p]) - _tile_lanes(cb, tk))
            acc_ref[...] += _dot(w.astype(bf16), v_ref[pl.ds(key0(t), tk), :])
            carry_ref[...] = cb + rs_ref[p]

        for d in reversed(range(tq // tk)):
            step(q0 + d * tk, tk, us, "strict", d * tk)

        @pl.when(m > 0)
        def _():
            scores(0, 0)
            scores(1, 1)
            sums(0)

            def k_body(jj, c2):
                scores(0, 2 * jj + 2)
                sums(1)
                values(0, 2 * jj)
                scores(1, 2 * jj + 3)
                sums(0)
                values(1, 2 * jj + 1)
                return c2

            lax.fori_loop(0, m - 1, k_body, 0)
            sums(1)
            values(0, 2 * m - 2)
            values(1, 2 * m - 1)

        step(0, BLOCK, us_meta, "valid", None)
        o_ref[pl.ds(q0, tq), :] = acc_ref[...].astype(o_ref.dtype)
        return c

    lax.fori_loop(0, n_tiles, q_body, 0)


def _sb_attention(qkv, heads):
    l = qkv.shape[0]
    n_tiles = (l - BLOCK) // ATT_Q_TILE
    assert BLOCK + n_tiles * ATT_Q_TILE == l
    idx = jnp.arange(ATT_K_TILE)
    us = (idx[:, None] > idx[None, :]).astype(bf16)
    blk = 4 * l * HEAD_DIM * 2 + ATT_K_TILE * ATT_K_TILE * 2
    kt = _key_blocks(qkv[:, heads * HEAD_DIM:2 * heads * HEAD_DIM])
    return pl.pallas_call(
        functools.partial(_sb_kernel, scale2=HEAD_DIM ** -0.5 * LOG2E, n_tiles=n_tiles),
        grid=(heads,),
        in_specs=[pl.BlockSpec((l, HEAD_DIM), lambda h: (0, h)),
                  pl.BlockSpec((None, l // BLOCK, HEAD_DIM, BLOCK), lambda h: (h, 0, 0, 0)),
                  pl.BlockSpec((l, HEAD_DIM), lambda h: (0, 2 * heads + h)),
                  pl.BlockSpec((ATT_K_TILE, ATT_K_TILE), lambda h: (0, 0))],
        out_specs=pl.BlockSpec((l, HEAD_DIM), lambda h: (0, h)),
        out_shape=jax.ShapeDtypeStruct((l, heads * HEAD_DIM), bf16),
        scratch_shapes=[pltpu.VMEM((ATT_Q_TILE, HEAD_DIM), f32), pltpu.VMEM((ATT_Q_TILE, BLOCK), f32),
                        pltpu.VMEM((2, ATT_Q_TILE, ATT_K_TILE), f32),
                        pltpu.VMEM((2, ATT_Q_TILE, ATT_K_TILE), f32),
                        pltpu.VMEM((2, ATT_Q_TILE, ATT_K_TILE), f32),
                        pltpu.VMEM((2, ATT_Q_TILE, BLOCK), f32)],
        compiler_params=pltpu.CompilerParams(
            dimension_semantics=("parallel",), vmem_limit_bytes=_vmem_limit(blk)),
        name="sb_attention",
    )(qkv, kt, qkv, us)


def _rope_mix(y2, cs):
    t = y2 * cs
    pe = t + pltpu.roll(t, ROPE_DIM, axis=1)
    lane = lax.broadcasted_iota(jnp.int32, pe.shape, 1)
    return jnp.where(lane < ROPE_DIM, pe, 0.0)


def _rms(x, g):
    ms = jnp.mean(x * x, axis=-1, keepdims=True)
    return x * lax.rsqrt(ms + RMS_EPS) * g


def _q_up_kernel(cq_ref, g_ref, w_ref, cs_ref, o_ref, xn_ref):
    @pl.when(pl.program_id(1) == 0)
    def _():
        xn_ref[...] = _rms(cq_ref[...], g_ref[...]).astype(bf16)

    y = _dot(xn_ref[...], w_ref[...])
    o_ref[:, 0:HEAD_DIM] = y[:, 0:HEAD_DIM].astype(bf16)
    o_ref[:, HEAD_DIM:] = _rope_mix(y[:, HEAD_DIM:], cs_ref[...]).astype(bf16)


def _kv_up_kernel(ckv_ref, g_ref, wk_ref, wv_ref, kpe_ref, cs_ref, k_ref, v_ref, xn_ref):
    @pl.when(pl.program_id(1) == 0)
    def _():
        xn_ref[...] = _rms(ckv_ref[...], g_ref[...]).astype(bf16)

    xn = xn_ref[...]
    k_ref[:, 0:HEAD_DIM] = _dot(xn, wk_ref[...]).astype(bf16)
    k_ref[:, HEAD_DIM:] = _rope_mix(kpe_ref[...], cs_ref[...]).astype(bf16)
    v_ref[:, 0:HEAD_DIM] = _dot(xn, wv_ref[...]).astype(bf16)
    v_ref[:, HEAD_DIM:] = jnp.ones((v_ref.shape[0], HEAD_DIM), bf16)


def _mla_softmax_pv(s, v1, m_old, mode, coff):
    allowed = _causal_mask(s.shape, mode, coff)
    if allowed is not None:
        s = jnp.where(allowed, s, NEG_INF)
    m_new = jnp.maximum(m_old, jnp.max(s, axis=1, keepdims=True))
    p = jnp.exp2(s - _tile_lanes(m_new, s.shape[1]))
    return m_new, jnp.exp2(m_old - m_new), _dot(p.astype(bf16), v1)


def _mla_kernel(q_ref, k_ref, v_ref, o_ref, m_ref, acc_ref, s_ref, *, scale2, n_tiles):
    tq, tk = ATT_Q_TILE, MLA_K_TILE
    assert tq == 2 * tk

    s = _dot(q_ref[0:BLOCK, :], k_ref[0]) * scale2
    _, _, pv = _mla_softmax_pv(s, v_ref[0:BLOCK, :], jnp.full((BLOCK, BLOCK), NEG_INF, f32), "causal_valid", 0)
    o_ref[0:BLOCK, :] = (pv[:, 0:HEAD_DIM] / pv[:, HEAD_DIM:]).astype(o_ref.dtype)

    def q_body(mi, c):
        q0 = pl.multiple_of(BLOCK + mi * tq, BLOCK)
        q = q_ref[pl.ds(q0, tq), :]

        def scores(t):
            return _dot(q, _load_kt(k_ref, pl.multiple_of(BLOCK + t * tk, BLOCK), tk)) * scale2

        def consume(s, v1, mode, coff):
            m_new, corr, pv = _mla_softmax_pv(s, v1, m_ref[...], mode, coff)
            acc_ref[...] = _tile_lanes(corr, 2 * HEAD_DIM) * acc_ref[...] + pv
            m_ref[...] = m_new

        def consume_tile(buf, t, mode, coff):
            consume(s_ref[buf], v_ref[pl.ds(pl.multiple_of(BLOCK + t * tk, BLOCK), tk), :], mode, coff)

        m_ref[...] = jnp.full(m_ref.shape, NEG_INF, f32)
        acc_ref[...] = jnp.zeros(acc_ref.shape, f32)
        consume(_dot(q, k_ref[0]) * scale2, v_ref[0:BLOCK, :], "valid", None)

        s_ref[0] = scores(0)

        def k_body(j, c2):
            s_ref[1] = scores(2 * j + 1)
            consume_tile(0, 2 * j, "full", None)
            s_ref[0] = scores(2 * j + 2)
            consume_tile(1, 2 * j + 1, "full", None)
            return c2

        lax.fori_loop(0, mi, k_body, 0)
        s_ref[1] = scores(2 * mi + 1)
        consume_tile(0, 2 * mi, "causal", 0)
        consume_tile(1, 2 * mi + 1, "causal", tk)
        acc = acc_ref[...]
        o_ref[pl.ds(q0, tq), :] = (acc[:, 0:HEAD_DIM] / acc[:, HEAD_DIM:]).astype(o_ref.dtype)
        return c

    lax.fori_loop(0, n_tiles, q_body, 0)


def _mla_branch(cqkv, kpe, cs, q_norm, kv_norm, w_q, w_k, w_v):
    l = cqkv.shape[0]
    heads, q_rank, _ = w_q.shape
    kv_rank = w_k.shape[1]
    assert q_rank % kv_rank == 0
    qk_dim = 2 * HEAD_DIM
    tm = _pick(l, (640, 512, 384, 256, 128))
    grid = (l // tm, heads)
    q_cat = pl.pallas_call(
        _q_up_kernel, grid=grid,
        in_specs=[pl.BlockSpec((tm, q_rank), lambda i, h: (i, 0)),
                  pl.BlockSpec((1, q_rank), lambda i, h: (0, 0)),
                  pl.BlockSpec((None, q_rank, qk_dim), lambda i, h: (h, 0, 0)),
                  pl.BlockSpec((tm, BLOCK), lambda i, h: (i, 0))],
        out_specs=pl.BlockSpec((None, tm, qk_dim), lambda i, h: (h, i, 0)),
        out_shape=jax.ShapeDtypeStruct((heads, l, qk_dim), bf16),
        scratch_shapes=[pltpu.VMEM((tm, q_rank), bf16)],
        compiler_params=pltpu.CompilerParams(
            dimension_semantics=("parallel", "arbitrary"),
            vmem_limit_bytes=_vmem_limit(tm * q_rank * 6 + q_rank * qk_dim * 2 + tm * qk_dim * 2 + tm * BLOCK * 4)),
        name="mla_q_up",
    )(cqkv, q_norm.reshape(1, q_rank), w_q, cs)
    k_cat, v_cat = pl.pallas_call(
        _kv_up_kernel, grid=grid,
        in_specs=[pl.BlockSpec((tm, kv_rank), lambda i, h: (i, q_rank // kv_rank)),
                  pl.BlockSpec((1, kv_rank), lambda i, h: (0, 0)),
                  pl.BlockSpec((None, kv_rank, HEAD_DIM), lambda i, h: (h, 0, 0)),
                  pl.BlockSpec((None, kv_rank, HEAD_DIM), lambda i, h: (h, 0, 0)),
                  pl.BlockSpec((tm, BLOCK), lambda i, h: (i, 0)),
                  pl.BlockSpec((tm, BLOCK), lambda i, h: (i, 0))],
        out_specs=[pl.BlockSpec((None, tm, qk_dim), lambda i, h: (h, i, 0)),
                   pl.BlockSpec((None, tm, 2 * HEAD_DIM), lambda i, h: (h, i, 0))],
        out_shape=[jax.ShapeDtypeStruct((heads, l, qk_dim), bf16),
                   jax.ShapeDtypeStruct((heads, l, 2 * HEAD_DIM), bf16)],
        scratch_shapes=[pltpu.VMEM((tm, kv_rank), bf16)],
        compiler_params=pltpu.CompilerParams(
            dimension_semantics=("parallel", "arbitrary"),
            vmem_limit_bytes=_vmem_limit(tm * kv_rank * 6 + 2 * kv_rank * HEAD_DIM * 2 + tm * BLOCK * 8
                                         + tm * (qk_dim + 2 * HEAD_DIM) * 2)),
        name="mla_kv_up",
    )(cqkv, kv_norm.reshape(1, kv_rank), w_k, w_v, kpe, cs)

    n_tiles = (l - BLOCK) // ATT_Q_TILE
    blk = 3 * l * qk_dim * 2 + l * HEAD_DIM * 2
    kt = k_cat.reshape(heads, l // BLOCK, BLOCK, qk_dim).transpose(0, 1, 3, 2)
    return pl.pallas_call(
        functools.partial(_mla_kernel, scale2=(HEAD_DIM + ROPE_DIM) ** -0.5 * LOG2E, n_tiles=n_tiles),
        grid=(heads,),
        in_specs=[pl.BlockSpec((None, l, qk_dim), lambda h: (h, 0, 0)),
                  pl.BlockSpec((None, l // BLOCK, qk_dim, BLOCK), lambda h: (h, 0, 0, 0)),
                  pl.BlockSpec((None, l, 2 * HEAD_DIM), lambda h: (h, 0, 0))],
        out_specs=pl.BlockSpec((l, HEAD_DIM), lambda h: (0, h)),
        out_shape=jax.ShapeDtypeStruct((l, heads * HEAD_DIM), bf16),
        scratch_shapes=[pltpu.VMEM((ATT_Q_TILE, BLOCK), f32), pltpu.VMEM((ATT_Q_TILE, 2 * HEAD_DIM), f32),
                        pltpu.VMEM((2, ATT_Q_TILE, MLA_K_TILE), f32)],
        compiler_params=pltpu.CompilerParams(
            dimension_semantics=("parallel",), vmem_limit_bytes=_vmem_limit(blk)),
        name="mla_attention",
    )(q_cat, kt, v_cat)


def _col_window(rows, col, width):
    start = col // BLOCK * BLOCK
    off = col - start
    return (pl.Element(rows), pl.Element(width + (BLOCK if off else 0))), start, off


def _ssd_kernel(xbc_ref, xprev_ref, z_ref, dtr_ref, dtrt_ref, cw_ref, cb_ref, dtb_ref, dtbt_ref,
                alog_ref, alogt_ref, dexp_ref, ng_ref, e64_ref, e128_ref, tril_ref, triu_ref,
                o_ref, ht_ref, xs_ref, bc_ref, xdt_ref, xdec_ref, y_ref, *, d_inner, heads, conv_k, z_off, x_off):
    c = pl.program_id(0)
    n = SSM_STATE
    pairs = heads // 2
    pairs_per_group = pairs // SSM_GROUPS
    gn = SSM_GROUPS * n

    @pl.when(c == 0)
    def _():
        ht_ref[...] = jnp.zeros(ht_ref.shape, f32)

    row = lax.broadcasted_iota(jnp.int32, (BLOCK, 1), 0)
    grow = c * BLOCK + row
    valid_r = grow >= PAD_FRONT

    d_xbc = cw_ref.shape[1]
    x = jnp.where(valid_r, xbc_ref[:, x_off:x_off + d_xbc], 0.0)
    xp = jnp.where(grow - BLOCK >= PAD_FRONT, xprev_ref[:, x_off:x_off + d_xbc], 0.0)
    acc = x * cw_ref[conv_k - 1:conv_k, :] + cb_ref[...]
    for s in range(1, conv_k):
        shifted = jnp.where(row >= s, pltpu.roll(x, s, axis=0), pltpu.roll(xp, s, axis=0))
        acc = acc + shifted * cw_ref[conv_k - 1 - s:conv_k - s, :]
    xc = acc * _sigmoid(acc)
    xs_ref[...] = xc[:, 0:d_inner]
    bc_ref[...] = xc[:, d_inner:]

    dt = jnp.where(valid_r, _softplus(dtr_ref[...] + dtb_ref[...]), 0.0)
    adt = -jnp.exp(alog_ref[...]) * dt
    a_cum = _dot3_rhs(tril_ref[...], adt)
    col = lax.broadcasted_iota(jnp.int32, (1, BLOCK), 1)
    dtt = jnp.where(c * BLOCK + col >= PAD_FRONT, _softplus(dtrt_ref[...] + dtbt_ref[...]), 0.0)
    a_cum_t = _dot3_lhs(-jnp.exp(alogt_ref[...]) * dtt, triu_ref[...])

    e64 = e64_ref[...]
    dt64 = _dot3_lhs(dt, e64)
    ac64 = _dot3_lhs(a_cum, e64)
    ac128 = _dot3_lhs(a_cum, e128_ref[...])
    a_last = ac64[BLOCK - 1:BLOCK, :]
    xdt = xs_ref[...] * dt64
    xdt_ref[...] = xdt.astype(bf16)
    xdec_ref[...] = (xdt * jnp.exp(a_last - ac64)).astype(bf16)
    eac = jnp.exp(ac64)
    chunk_decay = jnp.exp(a_last)

    li = lax.broadcasted_iota(jnp.int32, (BLOCK, BLOCK), 0)
    si = lax.broadcasted_iota(jnp.int32, (BLOCK, BLOCK), 1)
    causal = li >= si
    low_half = si < SSM_HEAD_DIM
    for g in range(SSM_GROUPS):
        bg = bc_ref[:, g * n:(g + 1) * n]
        cg = bc_ref[:, gn + g * n:gn + (g + 1) * n].astype(bf16)
        bgt = bg.T.astype(bf16)
        cb = _dot(cg, bgt)
        for jp in range(pairs_per_group):
            j = g * pairs_per_group + jp
            lanes = slice(j * BLOCK, (j + 1) * BLOCK)
            xpair = xdt_ref[:, lanes]
            ys = []
            for hh in (2 * j, 2 * j + 1):
                seg = ac128[:, hh * BLOCK:(hh + 1) * BLOCK] - a_cum_t[hh:hh + 1, :]
                lm = jnp.exp(jnp.where(causal, seg, NEG_INF))
                ys.append(_dot((cb * lm).astype(bf16), xpair))
            y_diag = jnp.where(low_half, ys[0], ys[1])
            ht = ht_ref[j]
            y_off = _dot(cg, ht.astype(bf16)) * eac[:, lanes]
            ht_ref[j] = ht * chunk_decay[:, lanes] + _dot(bgt, xdec_ref[:, lanes])
            y_ref[:, lanes] = y_diag + y_off + xs_ref[:, lanes] * dexp_ref[:, lanes]

    zz = z_ref[:, z_off:z_off + d_inner]
    y = y_ref[...] * (zz * _sigmoid(zz))
    gs = d_inner // SSM_GROUPS
    for g in range(SSM_GROUPS):
        yg = y[:, g * gs:(g + 1) * gs]
        o_ref[:, g * gs:(g + 1) * gs] = _rms(yg, ng_ref[:, g * gs:(g + 1) * gs]).astype(o_ref.dtype)


def _ssd_branch(proj, z_col, x_col, dtr, dtr_t, conv_w, conv_b, dt_bias, a_log, d_skip, norm_g):
    l = proj.shape[0]
    d_inner = norm_g.shape[0]
    d_xbc = conv_w.shape[1]
    heads = dt_bias.shape[0]
    z_blk, z_start, z_off = _col_window(BLOCK, z_col, d_inner)
    x_blk, x_start, x_off = _col_window(BLOCK, x_col, d_xbc)
    conv_k = conv_w.shape[0]
    assert heads <= BLOCK and heads % (2 * SSM_GROUPS) == 0 and d_inner == heads * SSM_HEAD_DIM
    assert d_xbc == d_inner + 2 * SSM_GROUPS * SSM_STATE
    nc = l // BLOCK
    pad = BLOCK - heads
    dtb = jnp.pad(dt_bias, (0, pad))
    alog = jnp.pad(a_log, (0, pad))
    hid = jnp.arange(BLOCK)
    e64 = ((jnp.arange(d_inner)[None, :] // SSM_HEAD_DIM) == hid[:, None]).astype(bf16)
    e128 = ((jnp.arange(heads * BLOCK)[None, :] // BLOCK) == hid[:, None]).astype(bf16)
    tril = (hid[:, None] >= hid[None, :]).astype(bf16)
    triu = (hid[:, None] <= hid[None, :]).astype(bf16)
    full = lambda shape: pl.BlockSpec(shape, lambda c: (0,) * len(shape))
    blk = (2 * BLOCK * d_xbc * 4 + BLOCK * d_inner * 4 + BLOCK * d_inner * 2 + 4 * BLOCK * BLOCK * 4
           + conv_k * d_xbc * 4 + BLOCK * (d_inner + heads * BLOCK) * 2)
    scratch = [pltpu.VMEM((heads // 2, SSM_STATE, BLOCK), f32),
               pltpu.VMEM((BLOCK, d_inner), f32),
               pltpu.VMEM((BLOCK, d_xbc - d_inner), f32),
               pltpu.VMEM((BLOCK, d_inner), bf16),
               pltpu.VMEM((BLOCK, d_inner), bf16),
               pltpu.VMEM((BLOCK, d_inner), f32)]
    scratch_bytes = (heads // 2) * SSM_STATE * BLOCK * 4 + BLOCK * (d_inner * 12 + (d_xbc - d_inner) * 4)
    return pl.pallas_call(
        functools.partial(_ssd_kernel, d_inner=d_inner, heads=heads, conv_k=conv_k, z_off=z_off, x_off=x_off),
        grid=(nc,),
        in_specs=[pl.BlockSpec(x_blk, lambda c: (pl.multiple_of(c * BLOCK, BLOCK), x_start)),
                  pl.BlockSpec(x_blk, lambda c: (pl.multiple_of(jnp.maximum(c - 1, 0) * BLOCK, BLOCK), x_start)),
                  pl.BlockSpec(z_blk, lambda c: (pl.multiple_of(c * BLOCK, BLOCK), z_start)),
                  pl.BlockSpec((BLOCK, BLOCK), lambda c: (c, 0)),
                  pl.BlockSpec((BLOCK, BLOCK), lambda c: (0, c)),
                  full((conv_k, d_xbc)), full((1, d_xbc)),
                  full((1, BLOCK)), full((BLOCK, 1)), full((1, BLOCK)), full((BLOCK, 1)),
                  full((1, d_inner)), full((1, d_inner)),
                  full((BLOCK, d_inner)), full((BLOCK, heads * BLOCK)),
                  full((BLOCK, BLOCK)), full((BLOCK, BLOCK))],
        out_specs=pl.BlockSpec((BLOCK, d_inner), lambda c: (c, 0)),
        out_shape=jax.ShapeDtypeStruct((l, d_inner), bf16),
        scratch_shapes=scratch,
        compiler_params=pltpu.CompilerParams(
            dimension_semantics=("arbitrary",), vmem_limit_bytes=_vmem_limit(blk + scratch_bytes)),
        name="ssd",
    )(proj, proj, proj, dtr, dtr_t, conv_w, conv_b.reshape(1, d_xbc),
      dtb.reshape(1, BLOCK), dtb.reshape(BLOCK, 1), alog.reshape(1, BLOCK), alog.reshape(BLOCK, 1),
      jnp.repeat(d_skip, SSM_HEAD_DIM).reshape(1, d_inner), norm_g.reshape(1, d_inner),
      e64, e128, tril, triu)


def _merge_kernel(oa_ref, ob_ref, oc_ref, w_ref, ga_ref, gb_ref, gc_ref, o_ref, *, g_off):
    tn = o_ref.shape[1]
    gate = lambda g_ref: _sigmoid(g_ref[:, g_off:g_off + tn])
    acc = gate(ga_ref) * _dot(oa_ref[...], w_ref[0])
    acc = acc + gate(gb_ref) * _dot(ob_ref[...], w_ref[1])
    acc = acc + gate(gc_ref) * _dot(oc_ref[...], w_ref[2])
    o_ref[...] = acc.astype(o_ref.dtype)


def _merge(o_a, o_b, o_c, w_br, proj, g_col):
    l, bw = o_a.shape
    d = w_br.shape[2]
    tm = _pick(l, (640, 512, 384, 256, 128))
    tn = _pick(d, (512, 256, 128))
    nb = d // tn
    o_spec = pl.BlockSpec((tm, bw), lambda i, j: (i, 0))
    g_blk, g_start, g_off = _col_window(tm, g_col, tn)
    g_spec = lambda b: pl.BlockSpec(
        g_blk, lambda i, j: (pl.multiple_of(i * tm, BLOCK), pl.multiple_of(g_start + b * d + j * tn, BLOCK)))
    blk = 3 * tm * bw * 2 + 3 * bw * tn * 2 + 3 * tm * (tn + BLOCK) * 4 + tm * tn * 2
    return pl.pallas_call(
        functools.partial(_merge_kernel, g_off=g_off), grid=(l // tm, nb),
        in_specs=[o_spec, o_spec, o_spec,
                  pl.BlockSpec((3, bw, tn), lambda i, j: (0, 0, j)),
                  g_spec(0), g_spec(1), g_spec(2)],
        out_specs=pl.BlockSpec((tm, tn), lambda i, j: (i, j)),
        out_shape=jax.ShapeDtypeStruct((l, d), bf16),
        compiler_params=pltpu.CompilerParams(
            dimension_semantics=("parallel", "parallel"), vmem_limit_bytes=_vmem_limit(blk)),
        name="merge",
    )(o_a, o_b, o_c, w_br, proj, proj, proj)


def _shift_rows(x, halo, s):
    r = pltpu.roll(x, s, axis=0)
    hr = pltpu.roll(halo, s, axis=0)
    row8 = lax.broadcasted_iota(jnp.int32, halo.shape, 0)
    top = jnp.where(row8 < s, hr, r[0:8, :])
    return jnp.concatenate([top, r[8:, :]], axis=0)


def _ffn_up_glu_kernel(a_ref, wg_ref, wv_ref, cwg_ref, cwv_ref, bg_ref, bv_ref, o_ref,
                       wgb_ref, wvb_ref, tail_g_ref, tail_v_ref, *, conv_k, tm):
    i = pl.program_id(1)

    @pl.when(i == 0)
    def _():
        wgb_ref[...] = wg_ref[...].astype(bf16)
        wvb_ref[...] = wv_ref[...].astype(bf16)
        tail_g_ref[...] = jnp.zeros(tail_g_ref.shape, f32)
        tail_v_ref[...] = jnp.zeros(tail_v_ref.shape, f32)

    a = a_ref[...]
    row = lax.broadcasted_iota(jnp.int32, (tm, 1), 0) + i * tm

    def conv(wb_ref, tail_ref, cw_ref, b_ref):
        u = jnp.where(row >= PAD_FRONT, _dot(a, wb_ref[...]), 0.0)
        halo = tail_ref[...]
        acc = u * cw_ref[conv_k - 1:conv_k, :] + b_ref[...]
        for s in range(1, conv_k):
            acc = acc + _shift_rows(u, halo, s) * cw_ref[conv_k - 1 - s:conv_k - s, :]
        tail_ref[...] = u[tm - 8:tm, :]
        return acc

    gate = conv(wgb_ref, tail_g_ref, cwg_ref, bg_ref)
    val = conv(wvb_ref, tail_v_ref, cwv_ref, bv_ref)
    o_ref[...] = (gate * _sigmoid(gate) * val).astype(o_ref.dtype)


def _ffn_up_glu(a, w_up_stack, layer, conv_w, conv_b):
    l, k = a.shape
    two_ff = w_up_stack.shape[2]
    d_ff = two_ff // 2
    conv_k = conv_w.shape[0]
    assert conv_k <= 8
    tm = _pick(l, (640, 512, 384, 256, 128))
    tc = _pick(d_ff, (512, 256, 128))
    nb = d_ff // tc
    cb = conv_b.reshape(1, two_ff)
    blk = tm * k * 2 + tm * tc * 2 + 2 * (conv_k + 1) * tc * 4
    scratch_bytes = 2 * k * tc * (4 + 2) + 2 * 8 * tc * 4
    return pl.pallas_call(
        functools.partial(_ffn_up_glu_kernel, conv_k=conv_k, tm=tm),
        grid=(nb, l // tm),
        in_specs=[pl.BlockSpec((tm, k), lambda j, i: (i, 0)),
                  pl.BlockSpec((None, k, tc), lambda j, i: (layer, 0, j), pipeline_mode=pl.Buffered(1)),
                  pl.BlockSpec((None, k, tc), lambda j, i: (layer, 0, nb + j), pipeline_mode=pl.Buffered(1)),
                  pl.BlockSpec((conv_k, tc), lambda j, i: (0, j)),
                  pl.BlockSpec((conv_k, tc), lambda j, i: (0, nb + j)),
                  pl.BlockSpec((1, tc), lambda j, i: (0, j)),
                  pl.BlockSpec((1, tc), lambda j, i: (0, nb + j))],
        out_specs=pl.BlockSpec((tm, tc), lambda j, i: (i, j)),
        out_shape=jax.ShapeDtypeStruct((l, d_ff), bf16),
        scratch_shapes=[pltpu.VMEM((k, tc), bf16), pltpu.VMEM((k, tc), bf16),
                        pltpu.VMEM((8, tc), f32), pltpu.VMEM((8, tc), f32)],
        compiler_params=pltpu.CompilerParams(
            dimension_semantics=("arbitrary", "arbitrary"),
            vmem_limit_bytes=_vmem_limit(blk) + scratch_bytes),
        name="ffn_up_glu",
    )(a, w_up_stack, w_up_stack, conv_w, conv_w, cb, cb)


def _rotate_half_cols(w):
    half = w.shape[-1] // 2
    return jnp.concatenate([-w[..., half:], w[..., :half]], axis=-1)


def _layer(h, hb, cs, p, alpha, stacks, layer):
    w_in_s, w_o_s, w_up_s, w_down_s = stacks
    (w_in, q_norm, kv_norm, w_uq, w_ukv, conv_w, conv_b, dt_bias, a_log, d_skip, ssm_norm, w_br,
     ln1_g, ln1_b, f_conv_w, f_conv_b, ln2_g, ln2_b) = p
    d = h.shape[1]
    bw = w_br.shape[1]
    sb_heads = bw // HEAD_DIM
    q_rank, kv_rank = q_norm.shape[0], kv_norm.shape[0]
    ssm_heads = dt_bias.shape[0]
    d_inner = ssm_norm.shape[0]
    d_xbc = conv_w.shape[1]
    mla_heads = w_uq.shape[1] // (HEAD_DIM + ROPE_DIM)

    sizes = (bw, bw, bw, q_rank, kv_rank, ROPE_DIM, d_inner, d_xbc, ssm_heads, 3 * d)
    offs = [0]
    for s in sizes:
        offs.append(offs[-1] + s)
    assert offs[-1] == w_in.shape[1]
    seg = lambda i: w_in[:, offs[i]:offs[i + 1]]
    w_kpe = seg(5)
    w_misc = jnp.concatenate([w_kpe, _rotate_half_cols(w_kpe), seg(8),
                              jnp.zeros((d, BLOCK - ssm_heads), f32)], axis=1)
    qkv = _matmul_w32(hb, w_in_s, layer, 0, offs[3], bf16, name="in_qkv")
    cqkv = _matmul_w32(hb, w_in_s, layer, offs[3], offs[5] - offs[3], f32, name="in_cqkv")
    misc = _matmul(hb, w_misc.astype(bf16), f32, name="in_misc")
    rest0 = offs[5]
    rest = _matmul_w32(hb, w_in_s, layer, rest0, -(-(offs[10] - rest0) // BLOCK) * BLOCK, f32, ragged=True,
                       name="in_rest")

    o_a = _sb_attention(qkv, sb_heads)

    wq = w_uq.reshape(q_rank, mla_heads, HEAD_DIM + ROPE_DIM)
    wq_pe = wq[..., HEAD_DIM:]
    wq = jnp.concatenate([wq[..., :HEAD_DIM], wq_pe, _rotate_half_cols(wq_pe)], axis=-1)
    wq = wq.transpose(1, 0, 2).astype(bf16)
    wkv = w_ukv.reshape(kv_rank, mla_heads, 2 * HEAD_DIM).transpose(1, 0, 2).astype(bf16)
    o_b = _mla_branch(cqkv, misc[:, :BLOCK], cs, q_norm, kv_norm, wq, wkv[..., :HEAD_DIM], wkv[..., HEAD_DIM:])

    dtr = misc[:, BLOCK:]
    o_c = _ssd_branch(rest, offs[6] - rest0, offs[7] - rest0, dtr, dtr.T, conv_w, conv_b, dt_bias, a_log,
                      d_skip, ssm_norm)

    merged = _merge(o_a, o_b, o_c, w_br.astype(bf16), rest, offs[9] - rest0)
    s1 = _matmul_w32(merged, w_o_s, layer, 0, d, f32, res=h, alpha=alpha, name="out_proj")
    h, hb = _layer_norm(s1, ln1_g, ln1_b)

    act = _ffn_up_glu(hb, w_up_s, layer, f_conv_w, f_conv_b)
    s2 = _matmul_w32(act, w_down_s, layer, 0, d, f32, res=h, alpha=alpha, name="ffn_down")
    return _layer_norm(s2, ln2_g, ln2_b)


def kernel(x, meta_tokens, ln_in_g, ln_in_b, w_in, mla_q_norm, mla_kv_norm, w_uq, w_ukv, ssm_conv_w, ssm_conv_b, ssm_dt_bias, ssm_a_log, ssm_d, ssm_norm, w_br, w_o, ln1_g, ln1_b, w_up, ffn_conv_w, ffn_conv_b, w_down, ln2_g, ln2_b):
    b, seq, d = x.shape
    assert b == 1 and meta_tokens.shape[0] == N_META
    depth = w_in.shape[0]
    alpha = (2 * depth) ** 0.25
    l = BLOCK + seq
    h0 = jnp.concatenate([jnp.zeros((PAD_FRONT, d), x.dtype), meta_tokens.astype(x.dtype), x[0]], axis=0)

    pos = jnp.maximum(jnp.arange(l) - PAD_FRONT, 0).astype(f32)
    inv_freq = 1.0 / (ROPE_THETA ** (jnp.arange(0, ROPE_DIM, 2, dtype=f32) / ROPE_DIM))
    ang = pos[:, None] * inv_freq[None, :]
    cos, sin = jnp.cos(ang), jnp.sin(ang)
    cs = jnp.concatenate([cos, cos, sin, sin], axis=1)

    h, hb = _layer_norm(h0, ln_in_g, ln_in_b)
    for i in range(depth):
        p = (w_in[i], mla_q_norm[i], mla_kv_norm[i], w_uq[i], w_ukv[i], ssm_conv_w[i], ssm_conv_b[i],
             ssm_dt_bias[i], ssm_a_log[i], ssm_d[i], ssm_norm[i], w_br[i], ln1_g[i], ln1_b[i],
             ffn_conv_w[i], ffn_conv_b[i], ln2_g[i], ln2_b[i])
        h, hb = _layer(h, hb, cs, p, alpha, (w_in, w_o, w_up, w_down), i)
    return h[BLOCK:][None]
```

```python
import functools
import math

import jax
import jax.numpy as jnp
from jax import lax
from jax.experimental import pallas as pl
from jax.experimental.pallas import tpu as pltpu

f32 = jnp.float32
bf16 = jnp.bfloat16

BLOCK = 128
N_META = 16
PAD_FRONT = BLOCK - N_META
HEAD_DIM = 128
ROPE_DIM = 64
SSM_HEAD_DIM = 64
SSM_GROUPS = 4
SSM_STATE = 128
ROPE_THETA = 10000.0
LN_EPS = 1e-5
RMS_EPS = 1e-6
NEG_INF = -1e30
ATT_Q_TILE = 512
ATT_K_TILE = 256
MLA_K_TILE = 256
LOG2E = 1.4426950408889634

V7X_VMEM_BYTES = 64 * 1024 * 1024
VMEM_INTERNAL_SCRATCH = 8 * 1024 * 1024


def _vmem_limit(block_bytes):
    return int(min(2 * block_bytes + VMEM_INTERNAL_SCRATCH, V7X_VMEM_BYTES - 4 * 1024 * 1024))


def _pick(n, cands):
    for c in cands:
        if n % c == 0:
            return c
    raise ValueError(f"no tile for {n}")


def _dot(a, b):
    return jnp.dot(a, b, preferred_element_type=f32)


def _split3(x):
    hi = x.astype(bf16)
    r = x - hi.astype(f32)
    mid = r.astype(bf16)
    lo = (r - mid.astype(f32)).astype(bf16)
    return hi, mid, lo


def _dot3_lhs(x, w):
    hi, mid, lo = _split3(x)
    return _dot(hi, w) + _dot(mid, w) + _dot(lo, w)


def _dot3_rhs(w, x):
    hi, mid, lo = _split3(x)
    return _dot(w, hi) + _dot(w, mid) + _dot(w, lo)


def _sigmoid(x):
    return 1.0 / (1.0 + jnp.exp(-x))


def _softplus(x):
    return jnp.maximum(x, 0.0) + jnp.log(1.0 + jnp.exp(-jnp.abs(x)))


def _mm_w32_kernel(a_ref, w_ref, *rest, alpha, n_valid):
    if len(rest) == 3:
        r_ref, o_ref, wb_ref = rest
    else:
        r_ref = None
        o_ref, wb_ref = rest

    j = pl.program_id(0)

    @pl.when(pl.program_id(1) == 0)
    def _():
        w = w_ref[...]
        if n_valid is not None:
            tn = w.shape[1]
            col = lax.broadcasted_iota(jnp.int32, (1, tn), 1) + j * tn
            w = jnp.where(col < n_valid, w, 0.0)
        wb_ref[...] = w.astype(bf16)

    acc = _dot(a_ref[...], wb_ref[...])
    if r_ref is not None:
        acc = alpha * r_ref[...] + acc
    o_ref[...] = acc.astype(o_ref.dtype)


def _matmul_w32(a, w_stack, layer, col0, n, out_dtype, res=None, alpha=1.0, ragged=False, name="mm32"):
    m, k = a.shape
    assert w_stack.shape[1] == k
    tm = _pick(m, (640, 512, 384, 256, 128))
    tn = _pick(col0, (512, 256, 128)) if ragged else _pick(math.gcd(n, col0) if col0 else n, (512, 256, 128))
    cb0 = col0 // tn
    w_bufs = 2 if k * tn * 4 <= 8 * 1024 * 1024 else 1
    in_specs = [pl.BlockSpec((tm, k), lambda j, i: (i, 0)),
                pl.BlockSpec((None, k, tn), lambda j, i: (layer, 0, cb0 + j), pipeline_mode=pl.Buffered(w_bufs))]
    args = [a, w_stack]
    blk = tm * k * 2 + tm * tn * jnp.dtype(out_dtype).itemsize
    if res is not None:
        in_specs.append(pl.BlockSpec((tm, tn), lambda j, i: (i, j)))
        args.append(res)
        blk += tm * tn * 4
    return pl.pallas_call(
        functools.partial(_mm_w32_kernel, alpha=alpha, n_valid=w_stack.shape[2] - col0 if ragged else None),
        grid=(pl.cdiv(n, tn), m // tm), in_specs=in_specs,
        out_specs=pl.BlockSpec((tm, tn), lambda j, i: (i, j)),
        out_shape=jax.ShapeDtypeStruct((m, n), out_dtype),
        scratch_shapes=[pltpu.VMEM((k, tn), bf16)],
        compiler_params=pltpu.CompilerParams(
            dimension_semantics=("arbitrary", "arbitrary"),
            vmem_limit_bytes=_vmem_limit(blk) + k * tn * (4 * w_bufs + 2)),
        name=name,
    )(*args)


def _ln_kernel(x_ref, g_ref, b_ref, o_ref, ob_ref):
    x = x_ref[...]
    mu = jnp.mean(x, axis=-1, keepdims=True)
    xc = x - mu
    var = jnp.mean(xc * xc, axis=-1, keepdims=True)
    y = xc * lax.rsqrt(var + LN_EPS) * g_ref[...] + b_ref[...]
    o_ref[...] = y
    ob_ref[...] = y.astype(bf16)


def _layer_norm(x, g, b):
    m, d = x.shape
    tr = _pick(m, (320, 256, 128))
    blk = tr * d * (4 + 4 + 2)
    return pl.pallas_call(
        _ln_kernel, grid=(m // tr,),
        in_specs=[pl.BlockSpec((tr, d), lambda i: (i, 0)),
                  pl.BlockSpec((1, d), lambda i: (0, 0)),
                  pl.BlockSpec((1, d), lambda i: (0, 0))],
        out_specs=[pl.BlockSpec((tr, d), lambda i: (i, 0)),
                   pl.BlockSpec((tr, d), lambda i: (i, 0))],
        out_shape=[jax.ShapeDtypeStruct((m, d), f32), jax.ShapeDtypeStruct((m, d), bf16)],
        compiler_params=pltpu.CompilerParams(
            dimension_semantics=("parallel",), vmem_limit_bytes=_vmem_limit(blk)),
        name="layer_norm",
    )(x, g.reshape(1, d), b.reshape(1, d))


def _tile_lanes(x, width):
    return x if width == BLOCK else jnp.concatenate([x] * (width // BLOCK), axis=1)


def _causal_mask(shape, mode, coff):
    if mode == "full":
        return None
    row = lax.broadcasted_iota(jnp.int32, shape, 0)
    col = lax.broadcasted_iota(jnp.int32, shape, 1)
    if mode == "valid":
        return col >= PAD_FRONT
    if mode.startswith("strict"):
        allowed = col + coff < row
    else:
        allowed = col + coff <= row
    if mode.endswith("valid"):
        allowed = allowed & (col >= PAD_FRONT)
    return allowed


def _key_blocks(x):
    l = x.shape[0]
    return x.reshape(l // BLOCK, BLOCK, -1, HEAD_DIM).transpose(2, 0, 3, 1)


def _load_kt(kt_ref, k0, ktile):
    kb = k0 // BLOCK
    return jnp.concatenate([kt_ref[kb + i] for i in range(ktile // BLOCK)], axis=1)


def _sb_step(q, kt, v, us, carry, mode, coff, scale2):
    zl, hi, lo, rs = _sb_softplus(_dot(q, kt) * scale2, mode, coff)
    return _sb_weights_pv(zl, hi, lo, v, us, carry, mode, coff), rs


def _sb_softplus(z, mode, coff):
    nabs = lax.bitcast_convert_type(lax.bitcast_convert_type(z, jnp.uint32) | jnp.uint32(0x80000000), f32)
    sp = jnp.maximum(z, 0.0) + jnp.log(1.0 + jnp.exp2(nabs)) * LOG2E
    zl = z - sp
    allowed = _causal_mask(z.shape, mode, coff)
    if allowed is not None:
        sp = jnp.where(allowed, sp, 0.0)
    hi = sp.astype(bf16)
    lo = (sp - hi.astype(f32)).astype(bf16)
    return zl, hi, lo, jnp.sum(sp, axis=1, keepdims=True)


def _sb_weights_pv(zl, hi, lo, v, us, carry, mode, coff):
    later = _dot(hi, us) + _dot(lo, us)
    w = jnp.exp2((zl - later) - _tile_lanes(carry, zl.shape[1]))
    allowed = _causal_mask(zl.shape, mode, coff)
    if allowed is not None:
        w = jnp.where(allowed, w, 0.0)
    return _dot(w.astype(bf16), v)


def _sb_kernel(q_ref, k_ref, v_ref, us_ref, o_ref, acc_ref, carry_ref, z_ref, zl_ref, later_ref, rs_ref,
               *, scale2, n_tiles):
    tq, tk = ATT_Q_TILE, ATT_K_TILE
    assert tq == 2 * tk
    us = us_ref[...]
    us_meta = us_ref[0:BLOCK, 0:BLOCK]

    pv, _ = _sb_step(q_ref[0:BLOCK, :], k_ref[0], v_ref[0:BLOCK, :], us_meta,
                     jnp.zeros((BLOCK, BLOCK), f32), "strict_valid", 0, scale2)
    o_ref[0:BLOCK, :] = pv.astype(o_ref.dtype)

    def q_body(m, c):
        q0 = pl.multiple_of(BLOCK + m * tq, BLOCK)
        q = q_ref[pl.ds(q0, tq), :]
        acc_ref[...] = jnp.zeros(acc_ref.shape, f32)
        carry_ref[...] = jnp.zeros(carry_ref.shape, f32)

        def step(k0, ktile, u, mode, coff):
            cb = carry_ref[...]
            pv, rs = _sb_step(q, _load_kt(k_ref, k0, ktile), v_ref[pl.ds(k0, ktile), :], u, cb, mode, coff, scale2)
            acc_ref[...] += pv
            carry_ref[...] = cb + rs

        def key0(t):
            return pl.multiple_of(q0 - (t + 1) * tk, BLOCK)

        def scores(p, t):
            z_ref[p] = _dot(q, _load_kt(k_ref, key0(t), tk)) * scale2

        def sums(p):
            zl, hi, lo, rs = _sb_softplus(z_ref[p], "full", None)
            zl_ref[p] = zl
            rs_ref[p] = jnp.broadcast_to(rs, (tq, BLOCK))
            later_ref[p] = _dot(hi, us) + _dot(lo, us)

        def values(p, t):
            cb = carry_ref[...]
            w = jnp.exp2((zl_ref[p] - later_ref[p]) - _tile_lanes(cb, tk))
            acc_ref[...] += _dot(w.astype(bf16), v_ref[pl.ds(key0(t), tk), :])
            carry_ref[...] = cb + rs_ref[p]

        for d in reversed(range(tq // tk)):
            step(q0 + d * tk, tk, us, "strict", d * tk)

        @pl.when(m > 0)
        def _():
            scores(0, 0)
            scores(1, 1)
            sums(0)

            def k_body(jj, c2):
                scores(0, 2 * jj + 2)
                sums(1)
                values(0, 2 * jj)
                scores(1, 2 * jj + 3)
                sums(0)
                values(1, 2 * jj + 1)
                return c2

            lax.fori_loop(0, m - 1, k_body, 0)
            sums(1)
            values(0, 2 * m - 2)
            values(1, 2 * m - 1)

        step(0, BLOCK, us_meta, "valid", None)
        o_ref[pl.ds(q0, tq), :] = acc_ref[...].astype(o_ref.dtype)
        return c

    lax.fori_loop(0, n_tiles, q_body, 0)


def _sb_attention(qkv, heads):
    l = qkv.shape[0]
    n_tiles = (l - BLOCK) // ATT_Q_TILE
    assert BLOCK + n_tiles * ATT_Q_TILE == l
    idx = jnp.arange(ATT_K_TILE)
    us = (idx[:, None] > idx[None, :]).astype(bf16)
    blk = 4 * l * HEAD_DIM * 2 + ATT_K_TILE * ATT_K_TILE * 2
    kt = _key_blocks(qkv[:, heads * HEAD_DIM:2 * heads * HEAD_DIM])
    return pl.pallas_call(
        functools.partial(_sb_kernel, scale2=HEAD_DIM ** -0.5 * LOG2E, n_tiles=n_tiles),
        grid=(heads,),
        in_specs=[pl.BlockSpec((l, HEAD_DIM), lambda h: (0, h)),
                  pl.BlockSpec((None, l // BLOCK, HEAD_DIM, BLOCK), lambda h: (h, 0, 0, 0)),
                  pl.BlockSpec((l, HEAD_DIM), lambda h: (0, 2 * heads + h)),
                  pl.BlockSpec((ATT_K_TILE, ATT_K_TILE), lambda h: (0, 0))],
        out_specs=pl.BlockSpec((l, HEAD_DIM), lambda h: (0, h)),
        out_shape=jax.ShapeDtypeStruct((l, heads * HEAD_DIM), bf16),
        scratch_shapes=[pltpu.VMEM((ATT_Q_TILE, HEAD_DIM), f32), pltpu.VMEM((ATT_Q_TILE, BLOCK), f32),
                        pltpu.VMEM((2, ATT_Q_TILE, ATT_K_TILE), f32),
                        pltpu.VMEM((2, ATT_Q_TILE, ATT_K_TILE), f32),
                        pltpu.VMEM((2, ATT_Q_TILE, ATT_K_TILE), f32),
                        pltpu.VMEM((2, ATT_Q_TILE, BLOCK), f32)],
        compiler_params=pltpu.CompilerParams(
            dimension_semantics=("parallel",), vmem_limit_bytes=_vmem_limit(blk)),
        name="sb_attention",
    )(qkv, kt, qkv, us)


def _rope_mix(y2, cs):
    t = y2 * cs
    pe = t + pltpu.roll(t, ROPE_DIM, axis=1)
    lane = lax.broadcasted_iota(jnp.int32, pe.shape, 1)
    return jnp.where(lane < ROPE_DIM, pe, 0.0)


def _rms(x, g):
    ms = jnp.mean(x * x, axis=-1, keepdims=True)
    return x * lax.rsqrt(ms + RMS_EPS) * g


def _q_up_kernel(cq_ref, g_ref, w_ref, cs_ref, o_ref, xn_ref):
    @pl.when(pl.program_id(1) == 0)
    def _():
        xn_ref[...] = _rms(cq_ref[...], g_ref[...]).astype(bf16)

    y = _dot(xn_ref[...], w_ref[...])
    o_ref[:, 0:HEAD_DIM] = y[:, 0:HEAD_DIM].astype(bf16)
    o_ref[:, HEAD_DIM:] = _rope_mix(y[:, HEAD_DIM:], cs_ref[...]).astype(bf16)


def _kv_up_kernel(ckv_ref, g_ref, wk_ref, wv_ref, kpe_ref, cs_ref, k_ref, v_ref, xn_ref):
    @pl.when(pl.program_id(1) == 0)
    def _():
        xn_ref[...] = _rms(ckv_ref[...], g_ref[...]).astype(bf16)

    xn = xn_ref[...]
    k_ref[:, 0:HEAD_DIM] = _dot(xn, wk_ref[...]).astype(bf16)
    k_ref[:, HEAD_DIM:] = _rope_mix(kpe_ref[...], cs_ref[...]).astype(bf16)
    v_ref[:, 0:HEAD_DIM] = _dot(xn, wv_ref[...]).astype(bf16)
    v_ref[:, HEAD_DIM:] = jnp.ones((v_ref.shape[0], HEAD_DIM), bf16)


def _mla_softmax_pv(s, v1, m_old, mode, coff):
    allowed = _causal_mask(s.shape, mode, coff)
    if allowed is not None:
        s = jnp.where(allowed, s, NEG_INF)
    m_new = jnp.maximum(m_old, jnp.max(s, axis=1, keepdims=True))
    p = jnp.exp2(s - _tile_lanes(m_new, s.shape[1]))
    return m_new, jnp.exp2(m_old - m_new), _dot(p.astype(bf16), v1)


def _mla_kernel(q_ref, k_ref, v_ref, o_ref, m_ref, acc_ref, s_ref, *, scale2, n_tiles):
    tq, tk = ATT_Q_TILE, MLA_K_TILE
    assert tq == 2 * tk

    s = _dot(q_ref[0:BLOCK, :], k_ref[0]) * scale2
    _, _, pv = _mla_softmax_pv(s, v_ref[0:BLOCK, :], jnp.full((BLOCK, BLOCK), NEG_INF, f32), "causal_valid", 0)
    o_ref[0:BLOCK, :] = (pv[:, 0:HEAD_DIM] / pv[:, HEAD_DIM:]).astype(o_ref.dtype)

    def q_body(mi, c):
        q0 = pl.multiple_of(BLOCK + mi * tq, BLOCK)
        q = q_ref[pl.ds(q0, tq), :]

        def scores(t):
            return _dot(q, _load_kt(k_ref, pl.multiple_of(BLOCK + t * tk, BLOCK), tk)) * scale2

        def consume(s, v1, mode, coff):
            m_new, corr, pv = _mla_softmax_pv(s, v1, m_ref[...], mode, coff)
            acc_ref[...] = _tile_lanes(corr, 2 * HEAD_DIM) * acc_ref[...] + pv
            m_ref[...] = m_new

        def consume_tile(buf, t, mode, coff):
            consume(s_ref[buf], v_ref[pl.ds(pl.multiple_of(BLOCK + t * tk, BLOCK), tk), :], mode, coff)

        m_ref[...] = jnp.full(m_ref.shape, NEG_INF, f32)
        acc_ref[...] = jnp.zeros(acc_ref.shape, f32)
        consume(_dot(q, k_ref[0]) * scale2, v_ref[0:BLOCK, :], "valid", None)

        s_ref[0] = scores(0)

        def k_body(j, c2):
            s_ref[1] = scores(2 * j + 1)
            consume_tile(0, 2 * j, "full", None)
            s_ref[0] = scores(2 * j + 2)
            consume_tile(1, 2 * j + 1, "full", None)
            return c2

        lax.fori_loop(0, mi, k_body, 0)
        s_ref[1] = scores(2 * mi + 1)
        consume_tile(0, 2 * mi, "causal", 0)
        consume_tile(1, 2 * mi + 1, "causal", tk)
        acc = acc_ref[...]
        o_ref[pl.ds(q0, tq), :] = (acc[:, 0:HEAD_DIM] / acc[:, HEAD_DIM:]).astype(o_ref.dtype)
        return c

    lax.fori_loop(0, n_tiles, q_body, 0)


def _mla_branch(cqkv, kpe, cs, q_norm, kv_norm, w_q, w_k, w_v):
    l = cqkv.shape[0]
    heads, q_rank, _ = w_q.shape
    kv_rank = w_k.shape[1]
    assert q_rank % kv_rank == 0
    qk_dim = 2 * HEAD_DIM
    tm = _pick(l, (640, 512, 384, 256, 128))
    grid = (l // tm, heads)
    q_cat = pl.pallas_call(
        _q_up_kernel, grid=grid,
        in_specs=[pl.BlockSpec((tm, q_rank), lambda i, h: (i, 0)),
                  pl.BlockSpec((1, q_rank), lambda i, h: (0, 0)),
                  pl.BlockSpec((None, q_rank, qk_dim), lambda i, h: (h, 0, 0)),
                  pl.BlockSpec((tm, BLOCK), lambda i, h: (i, 0))],
        out_specs=pl.BlockSpec((None, tm, qk_dim), lambda i, h: (h, i, 0)),
        out_shape=jax.ShapeDtypeStruct((heads, l, qk_dim), bf16),
        scratch_shapes=[pltpu.VMEM((tm, q_rank), bf16)],
        compiler_params=pltpu.CompilerParams(
            dimension_semantics=("parallel", "arbitrary"),
            vmem_limit_bytes=_vmem_limit(tm * q_rank * 6 + q_rank * qk_dim * 2 + tm * qk_dim * 2 + tm * BLOCK * 4)),
        name="mla_q_up",
    )(cqkv, q_norm.reshape(1, q_rank), w_q, cs)
    k_cat, v_cat = pl.pallas_call(
        _kv_up_kernel, grid=grid,
        in_specs=[pl.BlockSpec((tm, kv_rank), lambda i, h: (i, q_rank // kv_rank)),
                  pl.BlockSpec((1, kv_rank), lambda i, h: (0, 0)),
                  pl.BlockSpec((None, kv_rank, HEAD_DIM), lambda i, h: (h, 0, 0)),
                  pl.BlockSpec((None, kv_rank, HEAD_DIM), lambda i, h: (h, 0, 0)),
                  pl.BlockSpec((tm, BLOCK), lambda i, h: (i, 0)),
                  pl.BlockSpec((tm, BLOCK), lambda i, h: (i, 0))],
        out_specs=[pl.BlockSpec((None, tm, qk_dim), lambda i, h: (h, i, 0)),
                   pl.BlockSpec((None, tm, 2 * HEAD_DIM), lambda i, h: (h, i, 0))],
        out_shape=[jax.ShapeDtypeStruct((heads, l, qk_dim), bf16),
                   jax.ShapeDtypeStruct((heads, l, 2 * HEAD_DIM), bf16)],
        scratch_shapes=[pltpu.VMEM((tm, kv_rank), bf16)],
        compiler_params=pltpu.CompilerParams(
            dimension_semantics=("parallel", "arbitrary"),
            vmem_limit_bytes=_vmem_limit(tm * kv_rank * 6 + 2 * kv_rank * HEAD_DIM * 2 + tm * BLOCK * 8
                                         + tm * (qk_dim + 2 * HEAD_DIM) * 2)),
        name="mla_kv_up",
    )(cqkv, kv_norm.reshape(1, kv_rank), w_k, w_v, kpe, cs)

    n_tiles = (l - BLOCK) // ATT_Q_TILE
    blk = 3 * l * qk_dim * 2 + l * HEAD_DIM * 2
    kt = k_cat.reshape(heads, l // BLOCK, BLOCK, qk_dim).transpose(0, 1, 3, 2)
    return pl.pallas_call(
        functools.partial(_mla_kernel, scale2=(HEAD_DIM + ROPE_DIM) ** -0.5 * LOG2E, n_tiles=n_tiles),
        grid=(heads,),
        in_specs=[pl.BlockSpec((None, l, qk_dim), lambda h: (h, 0, 0)),
                  pl.BlockSpec((None, l // BLOCK, qk_dim, BLOCK), lambda h: (h, 0, 0, 0)),
                  pl.BlockSpec((None, l, 2 * HEAD_DIM), lambda h: (h, 0, 0))],
        out_specs=pl.BlockSpec((l, HEAD_DIM), lambda h: (0, h)),
        out_shape=jax.ShapeDtypeStruct((l, heads * HEAD_DIM), bf16),
        scratch_shapes=[pltpu.VMEM((ATT_Q_TILE, BLOCK), f32), pltpu.VMEM((ATT_Q_TILE, 2 * HEAD_DIM), f32),
                        pltpu.VMEM((2, ATT_Q_TILE, MLA_K_TILE), f32)],
        compiler_params=pltpu.CompilerParams(
            dimension_semantics=("parallel",), vmem_limit_bytes=_vmem_limit(blk)),
        name="mla_attention",
    )(q_cat, kt, v_cat)


def _col_window(rows, col, width):
    start = col // BLOCK * BLOCK
    off = col - start
    return (pl.Element(rows), pl.Element(width + (BLOCK if off else 0))), start, off


def _ssd_kernel(xbc_ref, xprev_ref, z_ref, dtr_ref, dtrt_ref, cw_ref, cb_ref, dtb_ref, dtbt_ref,
                alog_ref, alogt_ref, dexp_ref, ng_ref, e64_ref, e128_ref, tril_ref, triu_ref,
                o_ref, ht_ref, xs_ref, bc_ref, xdt_ref, xdec_ref, y_ref, *, d_inner, heads, conv_k, z_off, x_off):
    c = pl.program_id(0)
    n = SSM_STATE
    pairs = heads // 2
    pairs_per_group = pairs // SSM_GROUPS
    gn = SSM_GROUPS * n

    @pl.when(c == 0)
    def _():
        ht_ref[...] = jnp.zeros(ht_ref.shape, f32)

    row = lax.broadcasted_iota(jnp.int32, (BLOCK, 1), 0)
    grow = c * BLOCK + row
    valid_r = grow >= PAD_FRONT

    d_xbc = cw_ref.shape[1]
    x = jnp.where(valid_r, xbc_ref[:, x_off:x_off + d_xbc], 0.0)
    xp = jnp.where(grow - BLOCK >= PAD_FRONT, xprev_ref[:, x_off:x_off + d_xbc], 0.0)
    acc = x * cw_ref[conv_k - 1:conv_k, :] + cb_ref[...]
    for s in range(1, conv_k):
        shifted = jnp.where(row >= s, pltpu.roll(x, s, axis=0), pltpu.roll(xp, s, axis=0))
        acc = acc + shifted * cw_ref[conv_k - 1 - s:conv_k - s, :]
    xc = acc * _sigmoid(acc)
    xs_ref[...] = xc[:, 0:d_inner]
    bc_ref[...] = xc[:, d_inner:]

    dt = jnp.where(valid_r, _softplus(dtr_ref[...] + dtb_ref[...]), 0.0)
    adt = -jnp.exp(alog_ref[...]) * dt
    a_cum = _dot3_rhs(tril_ref[...], adt)
    col = lax.broadcasted_iota(jnp.int32, (1, BLOCK), 1)
    dtt = jnp.where(c * BLOCK + col >= PAD_FRONT, _softplus(dtrt_ref[...] + dtbt_ref[...]), 0.0)
    a_cum_t = _dot3_lhs(-jnp.exp(alogt_ref[...]) * dtt, triu_ref[...])

    e64 = e64_ref[...]
    dt64 = _dot3_lhs(dt, e64)
    ac64 = _dot3_lhs(a_cum, e64)
    ac128 = _dot3_lhs(a_cum, e128_ref[...])
    a_last = ac64[BLOCK - 1:BLOCK, :]
    xdt = xs_ref[...] * dt64
    xdt_ref[...] = xdt.astype(bf16)
    xdec_ref[...] = (xdt * jnp.exp(a_last - ac64)).astype(bf16)
    eac = jnp.exp(ac64)
    chunk_decay = jnp.exp(a_last)

    li = lax.broadcasted_iota(jnp.int32, (BLOCK, BLOCK), 0)
    si = lax.broadcasted_iota(jnp.int32, (BLOCK, BLOCK), 1)
    causal = li >= si
    low_half = si < SSM_HEAD_DIM
    for g in range(SSM_GROUPS):
        bg = bc_ref[:, g * n:(g + 1) * n]
        cg = bc_ref[:, gn + g * n:gn + (g + 1) * n].astype(bf16)
        bgt = bg.T.astype(bf16)
        cb = _dot(cg, bgt)
        for jp in range(pairs_per_group):
            j = g * pairs_per_group + jp
            lanes = slice(j * BLOCK, (j + 1) * BLOCK)
            xpair = xdt_ref[:, lanes]
            ys = []
            for hh in (2 * j, 2 * j + 1):
                seg = ac128[:, hh * BLOCK:(hh + 1) * BLOCK] - a_cum_t[hh:hh + 1, :]
                lm = jnp.exp(jnp.where(causal, seg, NEG_INF))
                ys.append(_dot((cb * lm).astype(bf16), xpair))
            y_diag = jnp.where(low_half, ys[0], ys[1])
            ht = ht_ref[j]
            y_off = _dot(cg, ht.astype(bf16)) * eac[:, lanes]
            ht_ref[j] = ht * chunk_decay[:, lanes] + _dot(bgt, xdec_ref[:, lanes])
            y_ref[:, lanes] = y_diag + y_off + xs_ref[:, lanes] * dexp_ref[:, lanes]

    zz = z_ref[:, z_off:z_off + d_inner]
    y = y_ref[...] * (zz * _sigmoid(zz))
    gs = d_inner // SSM_GROUPS
    for g in range(SSM_GROUPS):
        yg = y[:, g * gs:(g + 1) * gs]
        o_ref[:, g * gs:(g + 1) * gs] = _rms(yg, ng_ref[:, g * gs:(g + 1) * gs]).astype(o_ref.dtype)


def _ssd_branch(proj, z_col, x_col, dtr, dtr_t, conv_w, conv_b, dt_bias, a_log, d_skip, norm_g):
    l = proj.shape[0]
    d_inner = norm_g.shape[0]
    d_xbc = conv_w.shape[1]
    heads = dt_bias.shape[0]
    z_blk, z_start, z_off = _col_window(BLOCK, z_col, d_inner)
    x_blk, x_start, x_off = _col_window(BLOCK, x_col, d_xbc)
    conv_k = conv_w.shape[0]
    assert heads <= BLOCK and heads % (2 * SSM_GROUPS) == 0 and d_inner == heads * SSM_HEAD_DIM
    assert d_xbc == d_inner + 2 * SSM_GROUPS * SSM_STATE
    nc = l // BLOCK
    pad = BLOCK - heads
    dtb = jnp.pad(dt_bias, (0, pad))
    alog = jnp.pad(a_log, (0, pad))
    hid = jnp.arange(BLOCK)
    e64 = ((jnp.arange(d_inner)[None, :] // SSM_HEAD_DIM) == hid[:, None]).astype(bf16)
    e128 = ((jnp.arange(heads * BLOCK)[None, :] // BLOCK) == hid[:, None]).astype(bf16)
    tril = (hid[:, None] >= hid[None, :]).astype(bf16)
    triu = (hid[:, None] <= hid[None, :]).astype(bf16)
    full = lambda shape: pl.BlockSpec(shape, lambda c: (0,) * len(shape))
    blk = (2 * BLOCK * d_xbc * 4 + BLOCK * d_inner * 4 + BLOCK * d_inner * 2 + 4 * BLOCK * BLOCK * 4
           + conv_k * d_xbc * 4 + BLOCK * (d_inner + heads * BLOCK) * 2)
    scratch = [pltpu.VMEM((heads // 2, SSM_STATE, BLOCK), f32),
               pltpu.VMEM((BLOCK, d_inner), f32),
               pltpu.VMEM((BLOCK, d_xbc - d_inner), f32),
               pltpu.VMEM((BLOCK, d_inner), bf16),
               pltpu.VMEM((BLOCK, d_inner), bf16),
               pltpu.VMEM((BLOCK, d_inner), f32)]
    scratch_bytes = (heads // 2) * SSM_STATE * BLOCK * 4 + BLOCK * (d_inner * 12 + (d_xbc - d_inner) * 4)
    return pl.pallas_call(
        functools.partial(_ssd_kernel, d_inner=d_inner, heads=heads, conv_k=conv_k, z_off=z_off, x_off=x_off),
        grid=(nc,),
        in_specs=[pl.BlockSpec(x_blk, lambda c: (pl.multiple_of(c * BLOCK, BLOCK), x_start)),
                  pl.BlockSpec(x_blk, lambda c: (pl.multiple_of(jnp.maximum(c - 1, 0) * BLOCK, BLOCK), x_start)),
                  pl.BlockSpec(z_blk, lambda c: (pl.multiple_of(c * BLOCK, BLOCK), z_start)),
                  pl.BlockSpec((BLOCK, BLOCK), lambda c: (c, 0)),
                  pl.BlockSpec((BLOCK, BLOCK), lambda c: (0, c)),
                  full((conv_k, d_xbc)), full((1, d_xbc)),
                  full((1, BLOCK)), full((BLOCK, 1)), full((1, BLOCK)), full((BLOCK, 1)),
                  full((1, d_inner)), full((1, d_inner)),
                  full((BLOCK, d_inner)), full((BLOCK, heads * BLOCK)),
                  full((BLOCK, BLOCK)), full((BLOCK, BLOCK))],
        out_specs=pl.BlockSpec((BLOCK, d_inner), lambda c: (c, 0)),
        out_shape=jax.ShapeDtypeStruct((l, d_inner), bf16),
        scratch_shapes=scratch,
        compiler_params=pltpu.CompilerParams(
            dimension_semantics=("arbitrary",), vmem_limit_bytes=_vmem_limit(blk + scratch_bytes)),
        name="ssd",
    )(proj, proj, proj, dtr, dtr_t, conv_w, conv_b.reshape(1, d_xbc),
      dtb.reshape(1, BLOCK), dtb.reshape(BLOCK, 1), alog.reshape(1, BLOCK), alog.reshape(BLOCK, 1),
      jnp.repeat(d_skip, SSM_HEAD_DIM).reshape(1, d_inner), norm_g.reshape(1, d_inner),
      e64, e128, tril, triu)


def _merge_kernel(oa_ref, ob_ref, oc_ref, w_ref, ga_ref, gb_ref, gc_ref, o_ref, *, g_off):
    tn = o_ref.shape[1]
    gate = lambda g_ref: _sigmoid(g_ref[:, g_off:g_off + tn])
    acc = gate(ga_ref) * _dot(oa_ref[...], w_ref[0])
    acc = acc + gate(gb_ref) * _dot(ob_ref[...], w_ref[1])
    acc = acc + gate(gc_ref) * _dot(oc_ref[...], w_ref[2])
    o_ref[...] = acc.astype(o_ref.dtype)


def _merge(o_a, o_b, o_c, w_br, proj, g_col):
    l, bw = o_a.shape
    d = w_br.shape[2]
    tm = _pick(l, (640, 512, 384, 256, 128))
    tn = _pick(d, (512, 256, 128))
    nb = d // tn
    o_spec = pl.BlockSpec((tm, bw), lambda i, j: (i, 0))
    g_blk, g_start, g_off = _col_window(tm, g_col, tn)
    g_spec = lambda b: pl.BlockSpec(
        g_blk, lambda i, j: (pl.multiple_of(i * tm, BLOCK), pl.multiple_of(g_start + b * d + j * tn, BLOCK)))
    blk = 3 * tm * bw * 2 + 3 * bw * tn * 2 + 3 * tm * (tn + BLOCK) * 4 + tm * tn * 2
    return pl.pallas_call(
        functools.partial(_merge_kernel, g_off=g_off), grid=(l // tm, nb),
        in_specs=[o_spec, o_spec, o_spec,
                  pl.BlockSpec((3, bw, tn), lambda i, j: (0, 0, j)),
                  g_spec(0), g_spec(1), g_spec(2)],
        out_specs=pl.BlockSpec((tm, tn), lambda i, j: (i, j)),
        out_shape=jax.ShapeDtypeStruct((l, d), bf16),
        compiler_params=pltpu.CompilerParams(
            dimension_semantics=("parallel", "parallel"), vmem_limit_bytes=_vmem_limit(blk)),
        name="merge",
    )(o_a, o_b, o_c, w_br, proj, proj, proj)


def _shift_rows(x, halo, s):
    r = pltpu.roll(x, s, axis=0)
    hr = pltpu.roll(halo, s, axis=0)
    row8 = lax.broadcasted_iota(jnp.int32, halo.shape, 0)
    top = jnp.where(row8 < s, hr, r[0:8, :])
    return jnp.concatenate([top, r[8:, :]], axis=0)


def _ffn_up_glu_kernel(a_ref, wg_ref, wv_ref, cwg_ref, cwv_ref, bg_ref, bv_ref, o_ref,
                       wgb_ref, wvb_ref, tail_g_ref, tail_v_ref, *, conv_k, tm):
    i = pl.program_id(1)

    @pl.when(i == 0)
    def _():
        wgb_ref[...] = wg_ref[...].astype(bf16)
        wvb_ref[...] = wv_ref[...].astype(bf16)
        tail_g_ref[...] = jnp.zeros(tail_g_ref.shape, f32)
        tail_v_ref[...] = jnp.zeros(tail_v_ref.shape, f32)

    a = a_ref[...]
    row = lax.broadcasted_iota(jnp.int32, (tm, 1), 0) + i * tm

    def conv(wb_ref, tail_ref, cw_ref, b_ref):
        u = jnp.where(row >= PAD_FRONT, _dot(a, wb_ref[...]), 0.0)
        halo = tail_ref[...]
        acc = u * cw_ref[conv_k - 1:conv_k, :] + b_ref[...]
        for s in range(1, conv_k):
            acc = acc + _shift_rows(u, halo, s) * cw_ref[conv_k - 1 - s:conv_k - s, :]
        tail_ref[...] = u[tm - 8:tm, :]
        return acc

    gate = conv(wgb_ref, tail_g_ref, cwg_ref, bg_ref)
    val = conv(wvb_ref, tail_v_ref, cwv_ref, bv_ref)
    o_ref[...] = (gate * _sigmoid(gate) * val).astype(o_ref.dtype)


def _ffn_up_glu(a, w_up_stack, layer, conv_w, conv_b):
    l, k = a.shape
    two_ff = w_up_stack.shape[2]
    d_ff = two_ff // 2
    conv_k = conv_w.shape[0]
    assert conv_k <= 8
    tm = _pick(l, (640, 512, 384, 256, 128))
    tc = _pick(d_ff, (512, 256, 128))
    nb = d_ff // tc
    cb = conv_b.reshape(1, two_ff)
    blk = tm * k * 2 + tm * tc * 2 + 2 * (conv_k + 1) * tc * 4
    scratch_bytes = 2 * k * tc * (4 + 2) + 2 * 8 * tc * 4
    return pl.pallas_call(
        functools.partial(_ffn_up_glu_kernel, conv_k=conv_k, tm=tm),
        grid=(nb, l // tm),
        in_specs=[pl.BlockSpec((tm, k), lambda j, i: (i, 0)),
                  pl.BlockSpec((None, k, tc), lambda j, i: (layer, 0, j), pipeline_mode=pl.Buffered(1)),
                  pl.BlockSpec((None, k, tc), lambda j, i: (layer, 0, nb + j), pipeline_mode=pl.Buffered(1)),
                  pl.BlockSpec((conv_k, tc), lambda j, i: (0, j)),
                  pl.BlockSpec((conv_k, tc), lambda j, i: (0, nb + j)),
                  pl.BlockSpec((1, tc), lambda j, i: (0, j)),
                  pl.BlockSpec((1, tc), lambda j, i: (0, nb + j))],
        out_specs=pl.BlockSpec((tm, tc), lambda j, i: (i, j)),
        out_shape=jax.ShapeDtypeStruct((l, d_ff), bf16),
        scratch_shapes=[pltpu.VMEM((k, tc), bf16), pltpu.VMEM((k, tc), bf16),
                        pltpu.VMEM((8, tc), f32), pltpu.VMEM((8, tc), f32)],
        compiler_params=pltpu.CompilerParams(
            dimension_semantics=("arbitrary", "arbitrary"),
            vmem_limit_bytes=_vmem_limit(blk) + scratch_bytes),
        name="ffn_up_glu",
    )(a, w_up_stack, w_up_stack, conv_w, conv_w, cb, cb)


def _rotate_half_cols(w):
    half = w.shape[-1] // 2
    return jnp.concatenate([-w[..., half:], w[..., :half]], axis=-1)


def _layer(h, hb, cs, p, alpha, stacks, layer):
    w_in_s, w_o_s, w_up_s, w_down_s = stacks
    (q_norm, kv_norm, w_uq, w_ukv, conv_w, conv_b, dt_bias, a_log, d_skip, ssm_norm, w_br,
     ln1_g, ln1_b, f_conv_w, f_conv_b, ln2_g, ln2_b) = p
    d = h.shape[1]
    bw = w_br.shape[1]
    sb_heads = bw // HEAD_DIM
    q_rank, kv_rank = q_norm.shape[0], kv_norm.shape[0]
    ssm_heads = dt_bias.shape[0]
    d_inner = ssm_norm.shape[0]
    d_xbc = conv_w.shape[1]
    mla_heads = w_uq.shape[1] // (HEAD_DIM + ROPE_DIM)

    sizes = (bw, bw, bw, q_rank, kv_rank, ROPE_DIM, d_inner, d_xbc, ssm_heads, 3 * d)
    offs = [0]
    for s in sizes:
        offs.append(offs[-1] + s)
    assert offs[-1] == w_in_s.shape[2]
    seg = lambda i: w_in_s[layer, :, offs[i]:offs[i + 1]]
    w_kpe = seg(5)
    w_misc = jnp.concatenate([w_kpe, _rotate_half_cols(w_kpe), seg(8),
                              jnp.zeros((d, BLOCK - ssm_heads), f32)], axis=1)
    qkv = _matmul_w32(hb, w_in_s, layer, 0, offs[3], bf16, name="in_qkv")
    cqkv = _matmul_w32(hb, w_in_s, layer, offs[3], offs[5] - offs[3], f32, name="in_cqkv")
    misc = _matmul_w32(hb, w_misc[None], 0, 0, 2 * BLOCK, f32, name="in_misc")
    rest0 = offs[5]
    rest = _matmul_w32(hb, w_in_s, layer, rest0, -(-(offs[10] - rest0) // BLOCK) * BLOCK, f32, ragged=True,
                       name="in_rest")

    o_a = _sb_attention(qkv, sb_heads)

    wq = w_uq.reshape(q_rank, mla_heads, HEAD_DIM + ROPE_DIM)
    wq_pe = wq[..., HEAD_DIM:]
    wq = jnp.concatenate([wq[..., :HEAD_DIM], wq_pe, _rotate_half_cols(wq_pe)], axis=-1)
    wq = wq.transpose(1, 0, 2).astype(bf16)
    wkv = w_ukv.reshape(kv_rank, mla_heads, 2 * HEAD_DIM).transpose(1, 0, 2).astype(bf16)
    o_b = _mla_branch(cqkv, misc[:, :BLOCK], cs, q_norm, kv_norm, wq, wkv[..., :HEAD_DIM], wkv[..., HEAD_DIM:])

    dtr = misc[:, BLOCK:]
    o_c = _ssd_branch(rest, offs[6] - rest0, offs[7] - rest0, dtr, dtr.T, conv_w, conv_b, dt_bias, a_log,
                      d_skip, ssm_norm)

    merged = _merge(o_a, o_b, o_c, w_br.astype(bf16), rest, offs[9] - rest0)
    s1 = _matmul_w32(merged, w_o_s, layer, 0, d, f32, res=h, alpha=alpha, name="out_proj")
    h, hb = _layer_norm(s1, ln1_g, ln1_b)

    act = _ffn_up_glu(hb, w_up_s, layer, f_conv_w, f_conv_b)
    s2 = _matmul_w32(act, w_down_s, layer, 0, d, f32, res=h, alpha=alpha, name="ffn_down")
    return _layer_norm(s2, ln2_g, ln2_b)


def kernel(x, meta_tokens, ln_in_g, ln_in_b, w_in, mla_q_norm, mla_kv_norm, w_uq, w_ukv, ssm_conv_w, ssm_conv_b, ssm_dt_bias, ssm_a_log, ssm_d, ssm_norm, w_br, w_o, ln1_g, ln1_b, w_up, ffn_conv_w, ffn_conv_b, w_down, ln2_g, ln2_b):
    b, seq, d = x.shape
    assert b == 1 and meta_tokens.shape[0] == N_META
    depth = w_in.shape[0]
    alpha = (2 * depth) ** 0.25
    l = BLOCK + seq
    h0 = jnp.concatenate([jnp.zeros((PAD_FRONT, d), x.dtype), meta_tokens.astype(x.dtype), x[0]], axis=0)

    pos = jnp.maximum(jnp.arange(l) - PAD_FRONT, 0).astype(f32)
    inv_freq = 1.0 / (ROPE_THETA ** (jnp.arange(0, ROPE_DIM, 2, dtype=f32) / ROPE_DIM))
    ang = pos[:, None] * inv_freq[None, :]
    cos, sin = jnp.cos(ang), jnp.sin(ang)
    cs = jnp.concatenate([cos, cos, sin, sin], axis=1)

    h, hb = _layer_norm(h0, ln_in_g, ln_in_b)
    for i in range(depth):
        p = (mla_q_norm[i], mla_kv_norm[i], w_uq[i], w_ukv[i], ssm_conv_w[i], ssm_conv_b[i],
             ssm_dt_bias[i], ssm_a_log[i], ssm_d[i], ssm_norm[i], w_br[i], ln1_g[i], ln1_b[i],
             ffn_conv_w[i], ffn_conv_b[i], ln2_g[i], ln2_b[i])
        h, hb = _layer(h, hb, cs, p, alpha, (w_in, w_o, w_up, w_down), i)
    return h[BLOCK:][None]
```
